```python
import math
import jax, jax.numpy as jnp
from jax import lax
import numpy as np

D_MODEL = 1024
BATCH = 16
SEQ = 4096
DEPTH = 2
DEC_BATCH = 16
DEC_SEQ = 32
PAST_LEN = 1024

CHUNK = 64
Q_BLOCK = 128
EPS = 1e-6

HG_HEADS = 8
HG_DK = 128
HG_DV = D_MODEL // HG_HEADS
HG_W = HG_HEADS * HG_DK
HG_VW = HG_HEADS * HG_DV

FOX_HEADS = 8
FOX_HD = 128
FOX_W = FOX_HEADS * FOX_HD

DIFF_HEADS = 8
DIFF_QK = 64
DIFF_V = 2 * DIFF_QK
DIFF_QKW = DIFF_HEADS * 2 * DIFF_QK
DIFF_VW = DIFF_HEADS * DIFF_V

D_FF = ((8 * D_MODEL + 3 * 256 - 1) // (3 * 256)) * 256

IN_SPLITS = (HG_W, HG_W, HG_VW, HG_VW,
             FOX_W, FOX_W, FOX_W, FOX_HEADS,
             DIFF_QKW, DIFF_QKW, DIFF_VW,
             D_MODEL, D_MODEL, D_MODEL)
D_IN = sum(IN_SPLITS)

kernel_name = "hybrid_hgrn2_fox_diffattn_stream_step"


def rmsnorm(x, g):
    xf = x.astype(jnp.float32)
    var = jnp.mean(xf * xf, axis=-1, keepdims=True)
    return (xf * lax.rsqrt(var + EPS)).astype(x.dtype) * g


def hgrn_chunk(S0, q, logf, k, v):
    T = q.shape[1]
    L = jnp.cumsum(logf, axis=1)
    Lr = L[:, T // 2:T // 2 + 1]
    Lend = L[:, -1:]
    q_in = q * jnp.exp(L)
    q_r = q * jnp.exp(L - Lr)
    k_r = k * jnp.exp(Lr - L)
    k_end = k * jnp.exp(Lend - L)
    causal = jnp.tril(jnp.ones((T, T), dtype=bool))
    A = jnp.where(causal, jnp.einsum('bthd,bshd->bhts', q_r, k_r), 0.0)
    o = jnp.einsum('bthd,bhde->bthe', q_in, S0) + jnp.einsum('bhts,bshe->bthe', A, v)
    S_new = jnp.exp(Lend[:, 0])[..., None] * S0 + jnp.einsum('bshd,bshe->bhde', k_end, v)
    return S_new, o


def hgrn_prompt(q, logf, k, v):
    B, T = q.shape[:2]
    nc = T // CHUNK

    def to_chunks(a):
        return jnp.moveaxis(a.reshape(B, nc, CHUNK, *a.shape[2:]), 1, 0)

    S0 = jnp.zeros((B, HG_HEADS, HG_DK, HG_DV), jnp.float32)
    S, o = lax.scan(lambda S, xs: hgrn_chunk(S, *xs), S0,
                    (to_chunks(q), to_chunks(logf), to_chunks(k), to_chunks(v)))
    return jnp.moveaxis(o, 0, 1).reshape(B, T, HG_HEADS, HG_DV), S


def sweep_query_blocks(attend, q_arrays, qpos):
    B, T = q_arrays[0].shape[:2]
    nb = T // Q_BLOCK
    blocks = tuple(jnp.moveaxis(a.reshape(B, nb, Q_BLOCK, *a.shape[2:]), 1, 0) for a in q_arrays)
    o = lax.map(lambda args: attend(*args), blocks + (qpos.reshape(nb, Q_BLOCK),))
    return jnp.moveaxis(o, 0, 1).reshape(B, T, *o.shape[3:])


def fox_attend(q, qc, qpos, k, v, kc, kpos):
    s = jnp.einsum('bqhd,bkhd->bhqk', q, k).astype(jnp.float32) * (FOX_HD ** -0.5)
    s = s + jnp.swapaxes(qc, 1, 2)[..., :, None] - jnp.swapaxes(kc, 1, 2)[..., None, :]
    s = jnp.where(kpos[None, :] <= qpos[:, None], s, -jnp.inf)
    p = jax.nn.softmax(s, axis=-1)
    return jnp.einsum('bhqk,bkhd->bqhd', p.astype(v.dtype), v)


def diff_attend(q, qpos, k, v, kpos, lam, slopes):
    s = jnp.einsum('bqhmd,bkhmd->bmhqk', q, k).astype(jnp.float32) * (DIFF_QK ** -0.5)
    dist = jnp.abs(qpos[:, None] - kpos[None, :]).astype(jnp.float32)
    s = s - slopes[:, None, None] * dist
    visible = (kpos[None, :] // CHUNK) <= (qpos[:, None] // CHUNK)
    s = jnp.where(visible, s, -jnp.inf)
    p = jax.nn.softmax(s, axis=-1)
    w = p[:, 0] - lam * p[:, 1]
    return jnp.einsum('bhqk,bkhd->bqhd', w.astype(v.dtype), v)


def token_mix(h, past, lb, w_in, hg_norm, fox_f_bias, lq1, lk1, lq2, lk2, diff_norm,
              w_br_hgrn, w_br_fox, w_br_diff, w_out, lam_init):
    B, T, _ = h.shape
    z = h @ w_in
    offs = np.cumsum(IN_SPLITS)[:-1].tolist()
    (hq, hf, hi, hog, fq, fk, fv, ff, dq, dk, dv, g_hg, g_fox, g_diff) = jnp.split(z, offs, axis=-1)
    P = 0 if past is None else past[0].shape[1]
    qpos = P + jnp.arange(T)
    kpos = jnp.arange(P + T)

    zf = hf.astype(jnp.float32).reshape(B, T, HG_HEADS, HG_DK)
    lbh = lb.astype(jnp.float32).reshape(HG_HEADS, HG_DK)
    hg_logf = jnp.logaddexp(jnp.log(lbh), jnp.log1p(-lbh) + jax.nn.log_sigmoid(zf))
    hg_k = (1.0 - lbh) * jax.nn.sigmoid(-zf)
    hg_q = hq.astype(jnp.float32).reshape(B, T, HG_HEADS, HG_DK)
    hg_i = hi.astype(jnp.float32).reshape(B, T, HG_HEADS, HG_DV)
    if past is None:
        hg_o, hg_S = hgrn_prompt(hg_q, hg_logf, hg_k, hg_i)
    else:
        hg_S, hg_o = hgrn_chunk(past[5].astype(jnp.float32), hg_q, hg_logf, hg_k, hg_i)
    hg_o = rmsnorm(hg_o, hg_norm).reshape(B, T, HG_VW).astype(h.dtype) * jax.nn.silu(hog)

    fq4 = fq.reshape(B, T, FOX_HEADS, FOX_HD)
    fk4 = fk.reshape(B, T, FOX_HEADS, FOX_HD)
    fv4 = fv.reshape(B, T, FOX_HEADS, FOX_HD)
    fox_logf = jax.nn.log_sigmoid(ff.astype(jnp.float32) + fox_f_bias.astype(jnp.float32))
    if past is None:
        k_all, v_all, logf_all = fk4, fv4, fox_logf
    else:
        k_all = jnp.concatenate([past[0], fk4], axis=1)
        v_all = jnp.concatenate([past[1], fv4], axis=1)
        logf_all = jnp.concatenate([past[2].astype(jnp.float32), fox_logf], axis=1)
    c = jnp.cumsum(logf_all, axis=1)
    qc = c[:, P:]
    attend_fox = lambda qb, qcb, pb: fox_attend(qb, qcb, pb, k_all, v_all, c, kpos)
    if past is None:
        fox_o = sweep_query_blocks(attend_fox, (fq4, qc), qpos)
    else:
        fox_o = attend_fox(fq4, qc, qpos)

    dq5 = dq.reshape(B, T, DIFF_HEADS, 2, DIFF_QK)
    dk4 = dk.reshape(B, T, DIFF_HEADS, 2 * DIFF_QK)
    dv4 = dv.reshape(B, T, DIFF_HEADS, DIFF_V)
    if past is None:
        dk_all, dv_all = dk4, dv4
    else:
        dk_all = jnp.concatenate([past[3], dk4], axis=1)
        dv_all = jnp.concatenate([past[4], dv4], axis=1)
    dk5 = dk_all.reshape(B, P + T, DIFF_HEADS, 2, DIFF_QK)
    lam = (jnp.exp(jnp.sum(lq1.astype(jnp.float32) * lk1.astype(jnp.float32)))
           - jnp.exp(jnp.sum(lq2.astype(jnp.float32) * lk2.astype(jnp.float32))) + lam_init)
    slopes = jnp.exp2(-8.0 * jnp.arange(1, DIFF_HEADS + 1, dtype=jnp.float32) / DIFF_HEADS)
    attend_diff = lambda qb, pb: diff_attend(qb, pb, dk5, dv_all, kpos, lam, slopes)
    if past is None:
        diff_o = sweep_query_blocks(attend_diff, (dq5,), qpos)
    else:
        diff_o = attend_diff(dq5, qpos)
    diff_o = (rmsnorm(diff_o, diff_norm) * (1.0 - lam_init)).astype(h.dtype)

    mix = (jax.nn.sigmoid(g_hg) * (hg_o @ w_br_hgrn)
           + jax.nn.sigmoid(g_fox) * (fox_o.reshape(B, T, FOX_W) @ w_br_fox)
           + jax.nn.sigmoid(g_diff) * (diff_o.reshape(B, T, DIFF_VW) @ w_br_diff))
    y = mix @ w_out
    new_state = (fk4, fv4, fox_logf.astype(h.dtype), dk4, dv4, hg_S.astype(h.dtype))
    return y, new_state


def swiglu(h, w_gate_up, w_down):
    a, b = jnp.split(h @ w_gate_up, 2, axis=-1)
    return (jax.nn.silu(a) * b) @ w_down


def run_trunk(x, past, norm_mix, w_in, hg_lb_logits, hg_norm, fox_f_bias, diff_lam_q1, diff_lam_k1,
              diff_lam_q2, diff_lam_k2, diff_norm, w_br_hgrn, w_br_fox, w_br_diff, w_out,
              norm_ffn, w_gate_up, w_down, norm_final):
    cs = jnp.cumsum(jax.nn.softmax(hg_lb_logits.astype(jnp.float32), axis=0), axis=0)
    lb_all = cs - cs[0:1]
    per_layer = []
    for l in range(DEPTH):
        layer_past = None if past is None else tuple(a[l] for a in past)
        lam_init = 0.8 - 0.6 * math.exp(-0.3 * l)
        y, st = token_mix(rmsnorm(x, norm_mix[l]), layer_past, lb_all[l], w_in[l], hg_norm[l],
                          fox_f_bias[l], diff_lam_q1[l], diff_lam_k1[l], diff_lam_q2[l], diff_lam_k2[l],
                          diff_norm[l], w_br_hgrn[l], w_br_fox[l], w_br_diff[l], w_out[l], lam_init)
        x = x + y
        x = x + swiglu(rmsnorm(x, norm_ffn[l]), w_gate_up[l], w_down[l])
        per_layer.append(st)
    states = tuple(jnp.stack(s, axis=0) for s in zip(*per_layer))
    return rmsnorm(x, norm_final), states


def setup_inputs(seed: int = 0) -> dict:
    key = jax.random.key(seed)
    ks = jax.random.split(key, 32)

    def nrm(k, shape, scale):
        return jax.random.normal(k, shape, jnp.float32) * scale

    return {
        "x_prompt": nrm(ks[0], (BATCH, SEQ, D_MODEL), 1.0),
        "x_sample": nrm(ks[1], (DEC_BATCH, DEC_SEQ, D_MODEL), 1.0),
        "cache_fox_k": nrm(ks[2], (DEPTH, DEC_BATCH, PAST_LEN, FOX_HEADS, FOX_HD), 1.0),
        "cache_fox_v": nrm(ks[3], (DEPTH, DEC_BATCH, PAST_LEN, FOX_HEADS, FOX_HD), 1.0),
        "cache_fox_logf": jax.nn.log_sigmoid(nrm(ks[4], (DEPTH, DEC_BATCH, PAST_LEN, FOX_HEADS), 1.0)),
        "cache_diff_k": nrm(ks[5], (DEPTH, DEC_BATCH, PAST_LEN, DIFF_HEADS, 2 * DIFF_QK), 1.0),
        "cache_diff_v": nrm(ks[6], (DEPTH, DEC_BATCH, PAST_LEN, DIFF_HEADS, DIFF_V), 1.0),
        "state_hgrn": nrm(ks[7], (DEPTH, DEC_BATCH, HG_HEADS, HG_DK, HG_DV), 0.3),
        "norm_mix": 1.0 + nrm(ks[8], (DEPTH, D_MODEL), 0.01),
        "w_in": nrm(ks[9], (DEPTH, D_MODEL, D_IN), D_MODEL ** -0.5),
        "hg_lb_logits": nrm(ks[10], (DEPTH, HG_W), 0.5),
        "hg_norm": 1.0 + nrm(ks[11], (DEPTH, HG_DV), 0.01),
        "fox_f_bias": nrm(ks[12], (DEPTH, FOX_HEADS), 0.01),
        "diff_lam_q1": nrm(ks[13], (DEPTH, DIFF_QK), 0.1),
        "diff_lam_k1": nrm(ks[14], (DEPTH, DIFF_QK), 0.1),
        "diff_lam_q2": nrm(ks[15], (DEPTH, DIFF_QK), 0.1),
        "diff_lam_k2": nrm(ks[16], (DEPTH, DIFF_QK), 0.1),
        "diff_norm": 1.0 + nrm(ks[17], (DEPTH, DIFF_V), 0.01),
        "w_br_hgrn": nrm(ks[18], (DEPTH, HG_VW, D_MODEL), HG_VW ** -0.5),
        "w_br_fox": nrm(ks[19], (DEPTH, FOX_W, D_MODEL), FOX_W ** -0.5),
        "w_br_diff": nrm(ks[20], (DEPTH, DIFF_VW, D_MODEL), DIFF_VW ** -0.5),
        "w_out": nrm(ks[21], (DEPTH, D_MODEL, D_MODEL), D_MODEL ** -0.5),
        "norm_ffn": 1.0 + nrm(ks[22], (DEPTH, D_MODEL), 0.01),
        "w_gate_up": nrm(ks[23], (DEPTH, D_MODEL, 2 * D_FF), D_MODEL ** -0.5),
        "w_down": nrm(ks[24], (DEPTH, D_FF, D_MODEL), D_FF ** -0.5),
        "norm_final": 1.0 + nrm(ks[25], (D_MODEL,), 0.01),
    }


def reference(x_prompt, x_sample, cache_fox_k, cache_fox_v, cache_fox_logf, cache_diff_k, cache_diff_v,
              state_hgrn, norm_mix, w_in, hg_lb_logits, hg_norm, fox_f_bias, diff_lam_q1, diff_lam_k1,
              diff_lam_q2, diff_lam_k2, diff_norm, w_br_hgrn, w_br_fox, w_br_diff, w_out, norm_ffn,
              w_gate_up, w_down, norm_final):
    weights = (norm_mix, w_in, hg_lb_logits, hg_norm, fox_f_bias, diff_lam_q1, diff_lam_k1, diff_lam_q2,
               diff_lam_k2, diff_norm, w_br_hgrn, w_br_fox, w_br_diff, w_out, norm_ffn, w_gate_up, w_down,
               norm_final)
    y_prompt, (fk_p, fv_p, fl_p, dk_p, dv_p, hs_p) = run_trunk(x_prompt, None, *weights)
    past = (cache_fox_k, cache_fox_v, cache_fox_logf, cache_diff_k, cache_diff_v, state_hgrn)
    y_sample, (fk_s, fv_s, fl_s, dk_s, dv_s, hs_s) = run_trunk(x_sample, past, *weights)
    return (y_prompt, y_sample, fk_p, fv_p, fl_p, dk_p, dv_p, hs_p, fk_s, fv_s, fl_s, dk_s, dv_s, hs_s)
```

```python
import functools
import math

import jax
import jax.numpy as jnp
from jax import lax
from jax.experimental import pallas as pl
from jax.experimental.pallas import tpu as pltpu

EPS = 1e-6
CHUNK = 64
HEAD = 128
HEADS = 8
D_MODEL = HEADS * HEAD

Z_HQ, Z_HF, Z_HI, Z_HOG, Z_FQ, Z_DQ, Z_GHG, Z_GFOX, Z_GDIFF = range(9)
N_ZSLOTS = 9
N_STATE_GROUPS = 4

V7X_VMEM_LIMIT_BYTES = 56 * 1024 * 1024

_NT = (((1,), (1,)), ((), ()))
_TN = (((0,), (0,)), ((), ()))


def _params(*semantics):
    return pltpu.CompilerParams(dimension_semantics=semantics,
                                vmem_limit_bytes=V7X_VMEM_LIMIT_BYTES)


def _sigmoid(x):
    return 1.0 / (1.0 + jnp.exp(-x))


def _log_sigmoid(x):
    return jnp.minimum(x, 0.0) - jnp.log1p(jnp.exp(-jnp.abs(x)))


def _rms(x, g):
    var = jnp.mean(x * x, axis=-1, keepdims=True)
    return (x * lax.rsqrt(var + EPS)) * g


def _resident(shape):
    nd = len(shape)
    return pl.BlockSpec(shape, lambda *_: (0,) * nd)


def _proj_in_kernel(x_ref, g_ref, w_ref, wff_ref, fb_ref,
                    z_ref, fk_ref, fv_ref, dk_ref, dv_ref, fl_ref, h_ref):
    j = pl.program_id(1)

    @pl.when(j == 0)
    def _():
        h = _rms(x_ref[...], g_ref[...]).astype(jnp.bfloat16)
        h_ref[...] = h
        ff = jnp.dot(h, wff_ref[...], preferred_element_type=jnp.float32)
        fl_ref[...] = _log_sigmoid(ff + fb_ref[...])

    res = jnp.dot(h_ref[...], w_ref[...], preferred_element_type=jnp.float32)

    @pl.when(j < N_ZSLOTS)
    def _():
        z_ref[...] = res

    for s, ref in enumerate((fk_ref, fv_ref, dk_ref, dv_ref)):
        @pl.when(j == N_ZSLOTS + s)
        def _(ref=ref):
            ref[...] = res


def _proj_in(x, g, w, wff, fb, *, tm):
    n, d = x.shape
    ngroups = N_ZSLOTS + N_STATE_GROUPS
    state_spec = pl.BlockSpec((tm, d), lambda i, j: (i, 0))
    state_shape = jax.ShapeDtypeStruct((n, d), jnp.float32)
    return pl.pallas_call(
        _proj_in_kernel,
        grid=(n // tm, ngroups),
        in_specs=[
            pl.BlockSpec((tm, d), lambda i, j: (i, 0)),
            _resident((1, d)),
            pl.BlockSpec((d, d), lambda i, j: (0, j)),
            _resident((d, HEAD)),
            _resident((1, HEAD)),
        ],
        out_specs=[
            pl.BlockSpec((None, tm, d), lambda i, j: (jnp.minimum(j, N_ZSLOTS - 1), i, 0)),
            state_spec, state_spec, state_spec, state_spec,
            pl.BlockSpec((tm, HEAD), lambda i, j: (i, 0)),
        ],
        out_shape=[
            jax.ShapeDtypeStruct((N_ZSLOTS, n, d), jnp.float32),
            state_shape, state_shape, state_shape, state_shape,
            jax.ShapeDtypeStruct((n, HEAD), jnp.float32),
        ],
        scratch_shapes=[pltpu.VMEM((tm, d), jnp.bfloat16)],
        compiler_params=_params("parallel", "arbitrary"),
        name="proj_in",
    )(x, g, w, wff, fb)


def _hgrn_kernel(zq_ref, zf_ref, zi_ref, zog_ref, lb_ref, gn_ref, s0_ref,
                 o_ref, sout_ref, st_ref, *, chunk, heads):
    t = pl.program_id(1)
    tt = zq_ref.shape[0]

    @pl.when(t == 0)
    def _():
        for h in range(heads):
            st_ref[h] = s0_ref[0, h].T

    lb = lb_ref[...]
    log_lb = jnp.log(lb)
    log_1m_lb = jnp.log1p(-lb)
    gn = gn_ref[...]
    row = lax.broadcasted_iota(jnp.int32, (chunk, chunk), 0)
    col = lax.broadcasted_iota(jnp.int32, (chunk, chunk), 1)
    causal = row >= col
    tril = causal.astype(jnp.float32)
    mid = chunk // 2

    def one_chunk(c, carry):
        r0 = pl.multiple_of(c * chunk, chunk)
        rows = pl.ds(r0, chunk)
        zf = zf_ref[rows, :]
        b = log_1m_lb + _log_sigmoid(zf)
        logf = jnp.maximum(log_lb, b) + jnp.log1p(jnp.exp(-jnp.abs(log_lb - b)))
        kk = (1.0 - lb) * _sigmoid(-zf)
        L = jnp.dot(tril, logf, preferred_element_type=jnp.float32,
                    precision=lax.Precision.HIGHEST)
        for h in range(heads):
            sl = slice(h * HEAD, (h + 1) * HEAD)
            Lh = L[:, sl]
            Lr = Lh[mid:mid + 1, :]
            Lend = Lh[chunk - 1:chunk, :]
            q = zq_ref[rows, sl]
            k = kk[:, sl]
            v = zi_ref[rows, sl].astype(jnp.bfloat16)
            q_in = (q * jnp.exp(Lh)).astype(jnp.bfloat16)
            q_r = (q * jnp.exp(Lh - Lr)).astype(jnp.bfloat16)
            k_r = (k * jnp.exp(Lr - Lh)).astype(jnp.bfloat16)
            k_end = (k * jnp.exp(Lend - Lh)).astype(jnp.bfloat16)
            a = lax.dot_general(q_r, k_r, _NT, preferred_element_type=jnp.float32)
            a = jnp.where(causal, a, 0.0).astype(jnp.bfloat16)
            st = st_ref[h]
            o = (lax.dot_general(q_in, st.astype(jnp.bfloat16), _NT,
                                 preferred_element_type=jnp.float32)
                 + jnp.dot(a, v, preferred_element_type=jnp.float32))
            st_ref[h] = st * jnp.exp(Lend) + lax.dot_general(
                v, k_end, _TN, preferred_element_type=jnp.float32)
            og = zog_ref[rows, sl]
            o_ref[rows, sl] = (_rms(o, gn) * (og * _sigmoid(og))).astype(o_ref.dtype)
        return carry

    lax.fori_loop(0, tt // chunk, one_chunk, 0)

    @pl.when(t == pl.num_programs(1) - 1)
    def _():
        for h in range(heads):
            sout_ref[0, h] = st_ref[h].T


def _hgrn(z, lb, gn, s0, *, batch, seq, chunk, tt):
    _, n, d = z.shape
    heads = d // HEAD
    nt = seq // tt

    def zspec(slot):
        return pl.BlockSpec((None, tt, d), lambda b, t: (slot, b * nt + t, 0))

    state_spec = pl.BlockSpec((1, heads, HEAD, HEAD), lambda b, t: (b, 0, 0, 0))
    return pl.pallas_call(
        functools.partial(_hgrn_kernel, chunk=chunk, heads=heads),
        grid=(batch, nt),
        in_specs=[zspec(Z_HQ), zspec(Z_HF), zspec(Z_HI), zspec(Z_HOG),
                  _resident((1, d)), _resident((1, HEAD)), state_spec],
        out_specs=[pl.BlockSpec((tt, d), lambda b, t: (b * nt + t, 0)), state_spec],
        out_shape=[jax.ShapeDtypeStruct((n, d), jnp.bfloat16),
                   jax.ShapeDtypeStruct((batch, heads, HEAD, HEAD), jnp.float32)],
        scratch_shapes=[pltpu.VMEM((heads, HEAD, HEAD), jnp.float32)],
        compiler_params=_params("parallel", "arbitrary"),
        name="hgrn",
    )(z, z, z, z, lb, gn, s0)


def _cumsum_kernel(x_ref, o_ref, *, blk):
    nblk = x_ref.shape[0] // blk
    row = lax.broadcasted_iota(jnp.int32, (blk, blk), 0)
    col = lax.broadcasted_iota(jnp.int32, (blk, blk), 1)
    tril = (row >= col).astype(jnp.float32)

    def body(i, carry):
        rows = pl.ds(pl.multiple_of(i * blk, blk), blk)
        c = jnp.dot(tril, x_ref[rows, :], preferred_element_type=jnp.float32,
                    precision=lax.Precision.HIGHEST) + carry
        o_ref[rows, :] = c
        return c[blk - 1:blk, :]

    lax.fori_loop(0, nblk, body, jnp.zeros((1, x_ref.shape[1]), jnp.float32))


def _cumsum_time(x):
    return pl.pallas_call(
        functools.partial(_cumsum_kernel, blk=128),
        out_shape=jax.ShapeDtypeStruct(x.shape, jnp.float32),
        compiler_params=_params(),
        name="fox_cumsum",
    )(x)


def _fox_kernel(q_ref, k_ref, v_ref, cq_ref, ck_ref, o_ref, kb_ref, vb_ref, *, heads, past, tk):
    b, h, i = pl.program_id(0), pl.program_id(1), pl.program_id(2)
    tq = q_ref.shape[0]

    @pl.when(i == 0)
    def _():
        kb_ref[...] = k_ref[...].astype(jnp.bfloat16)
        vb_ref[...] = v_ref[...].astype(jnp.bfloat16)

    q = (q_ref[...] * (HEAD ** -0.5)).astype(jnp.bfloat16)
    lane = lax.broadcasted_iota(jnp.int32, cq_ref.shape, 1)
    cq = jnp.sum(jnp.where(lane == b * heads + h, cq_ref[...], 0.0), axis=1, keepdims=True)
    qpos = past + i * tq + lax.broadcasted_iota(jnp.int32, (tq, 1), 0)
    kiota = lax.broadcasted_iota(jnp.int32, (1, tk), 1)
    nk = (past + (i + 1) * tq + tk - 1) // tk

    def body(j, carry):
        m, l, acc = carry
        rows = pl.ds(pl.multiple_of(j * tk, tk), tk)
        s = lax.dot_general(q, kb_ref[rows, :], _NT, preferred_element_type=jnp.float32)
        s = s + cq - ck_ref[j]
        s = jnp.where(j * tk + kiota <= qpos, s, -jnp.inf)
        m_new = jnp.maximum(m, jnp.max(s, axis=1, keepdims=True))
        alpha = jnp.exp(m - m_new)
        p = jnp.exp(s - m_new)
        l = alpha * l + jnp.sum(p, axis=1, keepdims=True)
        acc = alpha * acc + jnp.dot(p.astype(jnp.bfloat16), vb_ref[rows, :],
                                    preferred_element_type=jnp.float32)
        return m_new, l, acc

    init = (jnp.full((tq, 1), -jnp.inf, jnp.float32), jnp.zeros((tq, 1), jnp.float32),
            jnp.zeros((tq, HEAD), jnp.float32))
    _, l, acc = lax.fori_loop(0, nk, body, init)
    o_ref[...] = (acc / l).astype(o_ref.dtype)


def _fox(q_src, q_slot, k, v, c_rows, c_cols, *, batch, heads, tq_total, tk_total, past, tq, tk):
    _, n, d = q_src.shape
    nq = tq_total // tq
    nkb = tk_total // tk
    kv_spec = pl.BlockSpec((tk_total, HEAD), lambda b, h, i: (b, h))
    return pl.pallas_call(
        functools.partial(_fox_kernel, heads=heads, past=past, tk=tk),
        grid=(batch, heads, nq),
        in_specs=[
            pl.BlockSpec((None, tq, HEAD), lambda b, h, i: (q_slot, b * nq + i, h)),
            kv_spec, kv_spec,
            pl.BlockSpec((tq, batch * heads), lambda b, h, i: (past // tq + i, 0)),
            pl.BlockSpec((None, nkb, 1, tk), lambda b, h, i: (b * heads + h, 0, 0, 0)),
        ],
        out_specs=pl.BlockSpec((tq, HEAD), lambda b, h, i: (b * nq + i, h)),
        out_shape=jax.ShapeDtypeStruct((n, d), jnp.bfloat16),
        scratch_shapes=[pltpu.VMEM((tk_total, HEAD), jnp.bfloat16),
                        pltpu.VMEM((tk_total, HEAD), jnp.bfloat16)],
        compiler_params=_params("parallel", "parallel", "arbitrary"),
        name="fox_attn",
    )(q_src, k, v, c_rows, c_cols)


def _diff_kernel(q_ref, k_ref, v_ref, slope_ref, lq1_ref, lk1_ref, lq2_ref, lk2_ref, gn_ref,
                 o_ref, kb_ref, vb_ref, *, past, k_valid, tk, lam_init):
    i = pl.program_id(2)
    tq = q_ref.shape[0]
    dq = HEAD // 2

    @pl.when(i == 0)
    def _():
        kb_ref[...] = k_ref[...].astype(jnp.bfloat16)
        vb_ref[...] = v_ref[...].astype(jnp.bfloat16)

    lam = (jnp.exp(jnp.sum(lq1_ref[...] * lk1_ref[...], axis=1, keepdims=True))
           - jnp.exp(jnp.sum(lq2_ref[...] * lk2_ref[...], axis=1, keepdims=True)) + lam_init)

    q = q_ref[...] * (dq ** -0.5)
    first = lax.broadcasted_iota(jnp.int32, (tq, HEAD), 1) < dq
    q1 = jnp.where(first, q, 0.0).astype(jnp.bfloat16)
    q2 = jnp.where(first, 0.0, q).astype(jnp.bfloat16)
    slope = slope_ref[...]
    qpos = past + i * tq + lax.broadcasted_iota(jnp.int32, (tq, 1), 0)
    qchunk = qpos // CHUNK
    kiota = lax.broadcasted_iota(jnp.int32, (1, tk), 1)
    kmax = jnp.minimum(((past + (i + 1) * tq - 1) // CHUNK + 1) * CHUNK, k_valid)
    nk = (kmax + tk - 1) // tk

    def softmax_step(s, m, l, acc, vblk):
        m_new = jnp.maximum(m, jnp.max(s, axis=1, keepdims=True))
        alpha = jnp.exp(m - m_new)
        p = jnp.exp(s - m_new)
        l = alpha * l + jnp.sum(p, axis=1, keepdims=True)
        acc = alpha * acc + jnp.dot(p.astype(jnp.bfloat16), vblk,
                                    preferred_element_type=jnp.float32)
        return m_new, l, acc

    def body(j, carry):
        m1, l1, a1, m2, l2, a2 = carry
        rows = pl.ds(pl.multiple_of(j * tk, tk), tk)
        kblk = kb_ref[rows, :]
        vblk = vb_ref[rows, :]
        kpos = j * tk + kiota
        bias = slope * jnp.abs(qpos - kpos).astype(jnp.float32)
        visible = jnp.logical_and(kpos // CHUNK <= qchunk, kpos < k_valid)
        s1 = lax.dot_general(q1, kblk, _NT, preferred_element_type=jnp.float32) - bias
        s2 = lax.dot_general(q2, kblk, _NT, preferred_element_type=jnp.float32) - bias
        s1 = jnp.where(visible, s1, -jnp.inf)
        s2 = jnp.where(visible, s2, -jnp.inf)
        m1, l1, a1 = softmax_step(s1, m1, l1, a1, vblk)
        m2, l2, a2 = softmax_step(s2, m2, l2, a2, vblk)
        return m1, l1, a1, m2, l2, a2

    neg = jnp.full((tq, 1), -jnp.inf, jnp.float32)
    zero1 = jnp.zeros((tq, 1), jnp.float32)
    zacc = jnp.zeros((tq, HEAD), jnp.float32)
    _, l1, a1, _, l2, a2 = lax.fori_loop(0, nk, body, (neg, zero1, zacc, neg, zero1, zacc))
    o = a1 / l1 - lam * (a2 / l2)
    o_ref[...] = (_rms(o, gn_ref[...]) * (1.0 - lam_init)).astype(o_ref.dtype)


def _diff(q_src, q_slot, k, v, slopes, lq1, lk1, lq2, lk2, gn, *, batch, heads, tq_total,
          tk_total, past, k_valid, tq, tk, lam_init):
    _, n, d = q_src.shape
    nq = tq_total // tq
    kv_spec = pl.BlockSpec((tk_total, HEAD), lambda b, h, i: (b, h))
    lam_spec = _resident((1, HEAD // 2))
    return pl.pallas_call(
        functools.partial(_diff_kernel, past=past, k_valid=k_valid, tk=tk, lam_init=lam_init),
        grid=(batch, heads, nq),
        in_specs=[
            pl.BlockSpec((None, tq, HEAD), lambda b, h, i: (q_slot, b * nq + i, h)),
            kv_spec, kv_spec,
            pl.BlockSpec((None, 1, tk), lambda b, h, i: (h, 0, 0)),
            lam_spec, lam_spec, lam_spec, lam_spec,
            _resident((1, HEAD)),
        ],
        out_specs=pl.BlockSpec((tq, HEAD), lambda b, h, i: (b * nq + i, h)),
        out_shape=jax.ShapeDtypeStruct((n, d), jnp.bfloat16),
        scratch_shapes=[pltpu.VMEM((tk_total, HEAD), jnp.bfloat16),
                        pltpu.VMEM((tk_total, HEAD), jnp.bfloat16)],
        compiler_params=_params("parallel", "parallel", "arbitrary"),
        name="diff_attn",
    )(q_src, k, v, slopes, lq1, lk1, lq2, lk2, gn)


def _merge_kernel(a1_ref, a2_ref, a3_ref, g1_ref, g2_ref, g3_ref, x_ref,
                  w1_ref, w2_ref, w3_ref, wo_ref, o_ref):
    def branch(a_ref, g_ref, w_ref):
        return _sigmoid(g_ref[...]) * jnp.dot(a_ref[...], w_ref[...],
                                              preferred_element_type=jnp.float32)

    mix = branch(a1_ref, g1_ref, w1_ref) + branch(a2_ref, g2_ref, w2_ref) + branch(a3_ref, g3_ref, w3_ref)
    y = jnp.dot(mix.astype(jnp.bfloat16), wo_ref[...], preferred_element_type=jnp.float32)
    o_ref[...] = x_ref[...] + y


def _merge(a1, a2, a3, z, x, w1, w2, w3, wo, *, tm):
    n, d = x.shape
    act = pl.BlockSpec((tm, d), lambda i: (i, 0))

    def gate(slot):
        return pl.BlockSpec((None, tm, d), lambda i: (slot, i, 0))

    wspec = _resident((d, d))
    return pl.pallas_call(
        _merge_kernel,
        grid=(n // tm,),
        in_specs=[act, act, act, gate(Z_GHG), gate(Z_GFOX), gate(Z_GDIFF), act,
                  wspec, wspec, wspec, wspec],
        out_specs=act,
        out_shape=jax.ShapeDtypeStruct((n, d), jnp.float32),
        compiler_params=_params("parallel"),
        name="merge_out",
    )(a1, a2, a3, z, z, z, x, w1, w2, w3, wo)


def _ffn_kernel(x_ref, g_ref, wgu_ref, wd_ref, o_ref, *, d_ff, fchunk):
    x = x_ref[...]
    h = _rms(x, g_ref[...]).astype(jnp.bfloat16)
    acc = x
    for c0 in range(0, d_ff, fchunk):
        a = jnp.dot(h, wgu_ref[:, c0:c0 + fchunk], preferred_element_type=jnp.float32)
        b = jnp.dot(h, wgu_ref[:, d_ff + c0:d_ff + c0 + fchunk], preferred_element_type=jnp.float32)
        act = (a * _sigmoid(a) * b).astype(jnp.bfloat16)
        acc = acc + jnp.dot(act, wd_ref[c0:c0 + fchunk, :], preferred_element_type=jnp.float32)
    o_ref[...] = acc


def _ffn(x, g, wgu, wd, *, tm):
    n, d = x.shape
    d_ff = wd.shape[0]
    fchunk = d_ff // 2
    assert fchunk % HEAD == 0
    act = pl.BlockSpec((tm, d), lambda i: (i, 0))
    return pl.pallas_call(
        functools.partial(_ffn_kernel, d_ff=d_ff, fchunk=fchunk),
        grid=(n // tm,),
        in_specs=[act, _resident((1, d)), _resident(wgu.shape), _resident(wd.shape)],
        out_specs=act,
        out_shape=jax.ShapeDtypeStruct((n, d), jnp.float32),
        compiler_params=_params("parallel"),
        name="ffn",
    )(x, g, wgu, wd)


def _final_norm_kernel(x_ref, g_ref, o_ref):
    o_ref[...] = _rms(x_ref[...], g_ref[...])


def _final_norm(x, g, *, tm):
    n, d = x.shape
    act = pl.BlockSpec((tm, d), lambda i: (i, 0))
    return pl.pallas_call(
        _final_norm_kernel,
        grid=(n // tm,),
        in_specs=[act, _resident((1, d))],
        out_specs=act,
        out_shape=jax.ShapeDtypeStruct((n, d), jnp.float32),
        compiler_params=_params("parallel"),
        name="final_norm",
    )(x, g)


def _prep_layer_weights(w_in_l, d):
    hd = HEADS
    splits = (d, d, d, d, d, d, d, hd, d, d, d, d, d, d)
    offs = [0]
    for s in splits:
        offs.append(offs[-1] + s)
    cols = [w_in_l[:, offs[k]:offs[k + 1]] for k in range(len(splits))]
    hq, hf, hi, hog, fq, fk, fv, ff, dq, dk, dv, g_hg, g_fox, g_diff = cols
    order = [hq, hf, hi, hog, fq, dq, g_hg, g_fox, g_diff, fk, fv, dk, dv]
    w = jnp.concatenate(order, axis=1).astype(jnp.bfloat16)
    wff = jnp.pad(ff, ((0, 0), (0, HEAD - hd))).astype(jnp.bfloat16)
    return w, wff


def _pick_tile(n, want):
    t = min(n, want)
    assert n % t == 0
    return t


def _trunk(x, past, wts, lb_all, slopes_by_tk):
    batch, seq, d = x.shape
    n = batch * seq
    heads = d // HEAD
    bh = batch * heads
    depth = len(wts)
    prompt = past is None
    plen = 0 if prompt else past[0].shape[2]
    tm = _pick_tile(n, 512)

    if prompt:
        chunk, tt = CHUNK, _pick_tile(seq, 512)
        tq, tk = _pick_tile(seq, 256), _pick_tile(seq, 256)
        tk_total = seq
    else:
        chunk, tt = seq, seq
        tq, tk = seq, 128
        tk_total = -(-(plen + seq) // tk) * tk
    k_valid = plen + seq
    pad = tk_total - k_valid
    slopes = slopes_by_tk(tk)

    xf = x.reshape(n, d)
    states = []
    for l in range(depth):
        w = wts[l]
        lam_init = 0.8 - 0.6 * math.exp(-0.3 * l)
        z, fk, fv, dk, dv, fl = _proj_in(xf, w["norm_mix"], w["w_in"], w["w_ff"], w["f_bias"], tm=tm)
        fox_logf = fl[:, :heads].reshape(batch, seq, heads)

        if prompt:
            s0 = jnp.zeros((batch, heads, HEAD, HEAD), jnp.float32)
        else:
            s0 = past[5][l]
        hg_o, hg_s = _hgrn(z, lb_all[l][None, :], w["hg_norm"], s0,
                           batch=batch, seq=seq, chunk=chunk, tt=tt)

        if prompt:
            k_all, v_all, dk_all, dv_all, logf_all = fk, fv, dk, dv, fox_logf
        else:
            def with_past(cache, new):
                new4 = new.reshape(batch, seq, heads, HEAD)
                full = jnp.concatenate([cache, new4], axis=1)
                full = jnp.pad(full, ((0, 0), (0, pad), (0, 0), (0, 0)))
                return full.reshape(batch * tk_total, d)
            k_all, v_all = with_past(past[0][l], fk), with_past(past[1][l], fv)
            dk_all, dv_all = with_past(past[3][l], dk), with_past(past[4][l], dv)
            logf_all = jnp.pad(jnp.concatenate([past[2][l], fox_logf], axis=1),
                               ((0, 0), (0, pad), (0, 0)))
        c_rows = _cumsum_time(logf_all.transpose(1, 0, 2).reshape(tk_total, bh))
        c_cols = c_rows.T.reshape(bh, tk_total // tk, 1, tk)
        fox_o = _fox(z, Z_FQ, k_all, v_all, c_rows, c_cols, batch=batch, heads=heads,
                     tq_total=seq, tk_total=tk_total, past=plen, tq=tq, tk=tk)

        diff_o = _diff(z, Z_DQ, dk_all, dv_all, slopes, w["lq1"], w["lk1"], w["lq2"], w["lk2"],
                       w["diff_norm"], batch=batch, heads=heads, tq_total=seq, tk_total=tk_total,
                       past=plen, k_valid=k_valid, tq=tq, tk=tk, lam_init=lam_init)

        xf = _merge(hg_o, fox_o, diff_o, z, xf, w["w_br_hgrn"], w["w_br_fox"], w["w_br_diff"],
                    w["w_out"], tm=tm)
        xf = _ffn(xf, w["norm_ffn"], w["w_gate_up"], w["w_down"], tm=tm)

        def heads4(a):
            return a.reshape(batch, seq, heads, HEAD)
        states.append((heads4(fk), heads4(fv), fox_logf, heads4(dk), heads4(dv), hg_s))

    y = _final_norm(xf, wts[0]["norm_final"], tm=tm).reshape(batch, seq, d)
    stacked = tuple(jnp.stack(s, axis=0) for s in zip(*states))
    return y, stacked


def kernel(x_prompt, x_sample, cache_fox_k, cache_fox_v, cache_fox_logf, cache_diff_k, cache_diff_v,
           state_hgrn, norm_mix, w_in, hg_lb_logits, hg_norm, fox_f_bias, diff_lam_q1, diff_lam_k1,
           diff_lam_q2, diff_lam_k2, diff_norm, w_br_hgrn, w_br_fox, w_br_diff, w_out, norm_ffn,
           w_gate_up, w_down, norm_final):
    depth, d, _ = w_in.shape
    bf = jnp.bfloat16
    wts = []
    for l in range(depth):
        w, wff = _prep_layer_weights(w_in[l], d)
        wts.append(dict(
            norm_mix=norm_mix[l][None, :], w_in=w, w_ff=wff,
            f_bias=jnp.pad(fox_f_bias[l], (0, HEAD - HEADS))[None, :],
            hg_norm=hg_norm[l][None, :],
            lq1=diff_lam_q1[l][None, :], lk1=diff_lam_k1[l][None, :],
            lq2=diff_lam_q2[l][None, :], lk2=diff_lam_k2[l][None, :],
            diff_norm=diff_norm[l][None, :],
            w_br_hgrn=w_br_hgrn[l].astype(bf), w_br_fox=w_br_fox[l].astype(bf),
            w_br_diff=w_br_diff[l].astype(bf), w_out=w_out[l].astype(bf),
            norm_ffn=norm_ffn[l][None, :],
            w_gate_up=w_gate_up[l].astype(bf), w_down=w_down[l].astype(bf),
            norm_final=norm_final[None, :],
        ))
    cs = jnp.cumsum(jax.nn.softmax(hg_lb_logits.astype(jnp.float32), axis=0), axis=0)
    lb_all = cs - cs[0:1]

    def slopes_by_tk(tk):
        s = jnp.exp2(-8.0 * jnp.arange(1, HEADS + 1, dtype=jnp.float32) / HEADS)
        return jnp.broadcast_to(s[:, None, None], (HEADS, 1, tk))

    y_p, st_p = _trunk(x_prompt, None, wts, lb_all, slopes_by_tk)
    past = (cache_fox_k, cache_fox_v, cache_fox_logf, cache_diff_k, cache_diff_v, state_hgrn)
    y_s, st_s = _trunk(x_sample, past, wts, lb_all, slopes_by_tk)
    return (y_p, y_s) + st_p + st_s
```

```python
import functools
import math

import jax
import jax.numpy as jnp
from jax import lax
from jax.experimental import pallas as pl
from jax.experimental.pallas import tpu as pltpu

EPS = 1e-6
CHUNK = 64
CHUNK_SHIFT = 6
assert 1 << CHUNK_SHIFT == CHUNK
HEAD = 128
HEADS = 8
HEADS_PER_STEP = 2
LOG2E = math.log2(math.e)

ZF_HQ, ZF_HF = range(2)
ZB_HI, ZB_HOG, ZB_FQ, ZB_DQ, ZB_GHG, ZB_GFOX, ZB_GDIFF = range(7)
N_ZF, N_ZB, N_STATE_GROUPS = 2, 7, 4

V7X_VMEM_LIMIT_BYTES = 56 * 1024 * 1024

_NT = (((1,), (1,)), ((), ()))
_TN = (((0,), (0,)), ((), ()))
BF16 = jnp.bfloat16
F32 = jnp.float32


def _params(*semantics):
    return pltpu.CompilerParams(dimension_semantics=semantics,
                                vmem_limit_bytes=V7X_VMEM_LIMIT_BYTES)


def _sigmoid(x):
    return 1.0 / (1.0 + jnp.exp(-x))


def _log_sigmoid(x):
    return jnp.minimum(x, 0.0) - jnp.log1p(jnp.exp(-jnp.abs(x)))


def _rms(x, g):
    var = jnp.mean(x * x, axis=-1, keepdims=True)
    return (x * lax.rsqrt(var + EPS)) * g


def _resident(shape):
    nd = len(shape)
    return pl.BlockSpec(shape, lambda *_: (0,) * nd)


def _split3(x):
    hi = x.astype(BF16).astype(F32)
    r = x - hi
    mid = r.astype(BF16).astype(F32)
    lo = (r - mid).astype(BF16).astype(F32)
    return hi, mid, lo


def _bias_lanes(pos, row_terms, one_lanes_first):
    t0, t1, t2 = row_terms
    base = 3 if one_lanes_first else 0
    ones_lo = 0 if one_lanes_first else 3
    out = jnp.where(pos == base, t0, jnp.where(pos == base + 1, t1, jnp.where(pos == base + 2, t2, 0.0)))
    return jnp.where(jnp.logical_and(pos >= ones_lo, pos < ones_lo + 3), 1.0, out)


def _proj_in_kernel(x_ref, g_ref, w_ref, wff_ref, fb_ref,
                    zf_ref, zb_ref, fk_ref, fv_ref, dk_ref, dv_ref, fl_ref, h_ref):
    j = pl.program_id(1)

    @pl.when(j == 0)
    def _():
        h = _rms(x_ref[...], g_ref[...]).astype(BF16)
        h_ref[...] = h
        ff = jnp.dot(h, wff_ref[...], preferred_element_type=F32)
        fl_ref[...] = _log_sigmoid(ff + fb_ref[...])

    res = jnp.dot(h_ref[...], w_ref[...], preferred_element_type=F32)

    @pl.when(j < N_ZF)
    def _():
        zf_ref[...] = res

    @pl.when(jnp.logical_and(j >= N_ZF, j < N_ZF + N_ZB))
    def _():
        zb_ref[...] = res.astype(BF16)

    for s, ref in enumerate((fk_ref, fv_ref, dk_ref, dv_ref)):
        @pl.when(j == N_ZF + N_ZB + s)
        def _(ref=ref):
            ref[...] = res


def _proj_in(x, g, w, wff, fb, *, tm):
    n, d = x.shape
    ngroups = N_ZF + N_ZB + N_STATE_GROUPS
    state_spec = pl.BlockSpec((tm, d), lambda i, j: (i, 0))
    state_shape = jax.ShapeDtypeStruct((n, d), F32)
    return pl.pallas_call(
        _proj_in_kernel,
        grid=(n // tm, ngroups),
        in_specs=[
            pl.BlockSpec((tm, d), lambda i, j: (i, 0)),
            _resident((1, d)),
            pl.BlockSpec((d, d), lambda i, j: (0, j)),
            _resident((d, HEAD)),
            _resident((1, HEAD)),
        ],
        out_specs=[
            pl.BlockSpec((None, tm, d), lambda i, j: (jnp.minimum(j, N_ZF - 1), i, 0)),
            pl.BlockSpec((None, tm, d), lambda i, j: (jnp.clip(j - N_ZF, 0, N_ZB - 1), i, 0)),
            state_spec, state_spec, state_spec, state_spec,
            pl.BlockSpec((tm, HEAD), lambda i, j: (i, 0)),
        ],
        out_shape=[
            jax.ShapeDtypeStruct((N_ZF, n, d), F32),
            jax.ShapeDtypeStruct((N_ZB, n, d), BF16),
            state_shape, state_shape, state_shape, state_shape,
            jax.ShapeDtypeStruct((n, HEAD), F32),
        ],
        scratch_shapes=[pltpu.VMEM((tm, d), BF16)],
        compiler_params=_params("parallel", "arbitrary"),
        name="proj_in",
    )(x, g, w, wff, fb)


def _hgrn_kernel(zq_ref, zf_ref, zi_ref, zog_ref, lb_ref, gn_ref, s0_ref,
                 o_ref, sout_ref, st_ref, *, chunk, heads):
    t = pl.program_id(1)
    tt = zq_ref.shape[0]

    @pl.when(t == 0)
    def _():
        for h in range(heads):
            st_ref[h] = s0_ref[0, h].T

    lb = lb_ref[...]
    log_lb = jnp.log(lb)
    log_1m_lb = jnp.log1p(-lb)
    gn = gn_ref[...]
    row = lax.broadcasted_iota(jnp.int32, (chunk, chunk), 0)
    col = lax.broadcasted_iota(jnp.int32, (chunk, chunk), 1)
    causal = row >= col
    tril = causal.astype(F32)
    mid = chunk // 2

    def one_chunk(c, carry):
        r0 = pl.multiple_of(c * chunk, chunk)
        rows = pl.ds(r0, chunk)
        zf = zf_ref[rows, :]
        b = log_1m_lb + _log_sigmoid(zf)
        logf = jnp.maximum(log_lb, b) + jnp.log1p(jnp.exp(-jnp.abs(log_lb - b)))
        kk = (1.0 - lb) * _sigmoid(-zf)
        L = jnp.dot(tril, logf, preferred_element_type=F32, precision=lax.Precision.HIGHEST)
        for h in range(heads):
            sl = slice(h * HEAD, (h + 1) * HEAD)
            Lh = L[:, sl]
            Lr = Lh[mid:mid + 1, :]
            Lend = Lh[chunk - 1:chunk, :]
            q = zq_ref[rows, sl]
            k = kk[:, sl]
            v = zi_ref[rows, sl]
            q_in = (q * jnp.exp(Lh)).astype(BF16)
            q_r = (q * jnp.exp(Lh - Lr)).astype(BF16)
            k_r = (k * jnp.exp(Lr - Lh)).astype(BF16)
            k_end = (k * jnp.exp(Lend - Lh)).astype(BF16)
            a = lax.dot_general(q_r, k_r, _NT, preferred_element_type=F32)
            a = jnp.where(causal, a, 0.0).astype(BF16)
            st = st_ref[h]
            o = (lax.dot_general(q_in, st.astype(BF16), _NT, preferred_element_type=F32)
                 + jnp.dot(a, v, preferred_element_type=F32))
            st_ref[h] = st * jnp.exp(Lend) + lax.dot_general(v, k_end, _TN, preferred_element_type=F32)
            og = zog_ref[rows, sl].astype(F32)
            o_ref[rows, sl] = (_rms(o, gn) * (og * _sigmoid(og))).astype(o_ref.dtype)
        return carry

    lax.fori_loop(0, tt // chunk, one_chunk, 0)

    @pl.when(t == pl.num_programs(1) - 1)
    def _():
        for h in range(heads):
            sout_ref[0, h] = st_ref[h].T


def _hgrn(zf, zb, lb, gn, s0, *, batch, seq, chunk, tt):
    _, n, d = zf.shape
    heads = d // HEAD
    nt = seq // tt

    def zspec(slot):
        return pl.BlockSpec((None, tt, d), lambda b, t: (slot, b * nt + t, 0))

    state_spec = pl.BlockSpec((1, heads, HEAD, HEAD), lambda b, t: (b, 0, 0, 0))
    return pl.pallas_call(
        functools.partial(_hgrn_kernel, chunk=chunk, heads=heads),
        grid=(batch, nt),
        in_specs=[zspec(ZF_HQ), zspec(ZF_HF), zspec(ZB_HI), zspec(ZB_HOG),
                  _resident((1, d)), _resident((1, HEAD)), state_spec],
        out_specs=[pl.BlockSpec((tt, d), lambda b, t: (b * nt + t, 0)), state_spec],
        out_shape=[jax.ShapeDtypeStruct((n, d), BF16),
                   jax.ShapeDtypeStruct((batch, heads, HEAD, HEAD), F32)],
        scratch_shapes=[pltpu.VMEM((heads, HEAD, HEAD), F32)],
        compiler_params=_params("parallel", "arbitrary"),
        name="hgrn",
    )(zf, zf, zb, zb, lb, gn, s0)


def _cumsum_kernel(x_ref, o_ref, *, blk):
    nblk = x_ref.shape[0] // blk
    row = lax.broadcasted_iota(jnp.int32, (blk, blk), 0)
    col = lax.broadcasted_iota(jnp.int32, (blk, blk), 1)
    tril = (row >= col).astype(F32)

    def body(i, carry):
        rows = pl.ds(pl.multiple_of(i * blk, blk), blk)
        c = jnp.dot(tril, x_ref[rows, :], preferred_element_type=F32,
                    precision=lax.Precision.HIGHEST) + carry
        o_ref[rows, :] = c
        return c[blk - 1:blk, :]

    lax.fori_loop(0, nblk, body, jnp.zeros((1, x_ref.shape[1]), F32))


def _cumsum_time(x):
    return pl.pallas_call(
        functools.partial(_cumsum_kernel, blk=128),
        out_shape=jax.ShapeDtypeStruct(x.shape, F32),
        compiler_params=_params(),
        name="fox_cumsum",
    )(x)


STRIP = 64


def _online_softmax_block(c, q_aug, kt_blk, vblk, refs, fix_scores=None, aligned_diagonal=False):
    s_ref, p_ref, m_ref, acc_ref = refs
    tq, tk = s_ref.shape[1], s_ref.shape[2]
    strip = min(tq, STRIP)
    s_ref[c] = jnp.dot(q_aug, kt_blk, preferred_element_type=F32)
    for r0 in range(0, tq, strip):
        rs = slice(r0, r0 + strip)
        if aligned_diagonal:
            lo = r0 // HEAD * HEAD
            hi = (r0 + strip - 1) // HEAD * HEAD + HEAD
        else:
            lo, hi = 0, tk
        s = s_ref[c, rs, :hi]
        if fix_scores is not None:
            fixed = fix_scores(s[:, lo:], r0, strip, lo, hi - lo)
            s = fixed if lo == 0 else jnp.concatenate([s[:, :lo], fixed], axis=1)
        m_prev = m_ref[c, rs, :]
        m_new = jnp.maximum(m_prev, jnp.max(s, axis=1, keepdims=True))
        alpha = jnp.exp2(m_prev - m_new)
        p = jnp.exp2(s - pltpu.repeat(m_new, hi // HEAD, axis=1))
        m_ref[c, rs, :] = m_new
        acc_ref[c, rs, :] = pltpu.repeat(alpha, 2, axis=1) * acc_ref[c, rs, :]
        p_ref[c, rs, :hi] = p.astype(BF16)
        if hi < tk:
            p_ref[c, rs, hi:] = jnp.zeros((strip, tk - hi), BF16)
    acc_ref[c] += jnp.dot(p_ref[c], vblk, preferred_element_type=F32)


def _softmax_init(refs):
    _, _, m_ref, acc_ref = refs
    m_ref[...] = jnp.full(m_ref.shape, -jnp.inf, F32)
    acc_ref[...] = jnp.zeros(acc_ref.shape, F32)


def _softmax_result(refs, c):
    acc_ref = refs[3]
    return acc_ref[c, :, :HEAD] / acc_ref[c, :, HEAD:]


def _values_with_ones(v_ref, vb_ref):
    rows = v_ref.shape[0]
    for hh in range(v_ref.shape[1] // HEAD):
        vb_ref[:, 2 * hh * HEAD:(2 * hh + 1) * HEAD] = v_ref[:, hh * HEAD:(hh + 1) * HEAD].astype(BF16)
        vb_ref[:, (2 * hh + 1) * HEAD:(2 * hh + 2) * HEAD] = jnp.ones((rows, HEAD), BF16)


def _softmax_scratch(chains, tq, tk):
    return [pltpu.VMEM((chains, tq, tk), F32), pltpu.VMEM((chains, tq, tk), BF16),
            pltpu.VMEM((chains, tq, HEAD), F32), pltpu.VMEM((chains, tq, 2 * HEAD), F32)]


def _loop_two_per_trip(lo, hi, body):
    pairs = (hi - lo) // 2

    def two(t, carry):
        j = lo + 2 * t
        body(j, 0)
        body(j + 1, 0)
        return carry

    lax.fori_loop(0, pairs, two, 0)
    tail = lo + 2 * pairs

    @pl.when(tail < hi)
    def _():
        body(tail, 0)


def _block_counts(past, i, tq, tk, k_valid):
    n_before = (past + i * tq) // tk
    kmax = jnp.minimum(((past + (i + 1) * tq - 1) // CHUNK + 1) * CHUNK, k_valid)
    return n_before, (kmax + tk - 1) // tk


def _fox_kernel(q_ref, k_ref, v_ref, cq_ref, ck_ref, o_ref, kt_ref, vb_ref, *refs,
                heads, past, k_valid, tk):
    b, hp, i = pl.program_id(0), pl.program_id(1), pl.program_id(2)
    tq = q_ref.shape[0]
    hps = q_ref.shape[1] // HEAD
    nkb = kt_ref.shape[1]
    sub = tk // HEAD

    @pl.when(i == 0)
    def _():
        _values_with_ones(v_ref, vb_ref)
        row = lax.broadcasted_iota(jnp.int32, (HEAD, tk), 0)
        for hh in range(hps):
            def fill(jb, carry, hh=hh):
                r0 = pl.multiple_of(jb * tk, tk)
                for u in range(sub):
                    kblk = k_ref[pl.ds(r0 + u * HEAD, HEAD), hh * HEAD:(hh + 1) * HEAD]
                    kt_ref[hh, jb, 0:HEAD, u * HEAD:(u + 1) * HEAD] = kblk.T.astype(BF16)
                col_bias = _split3(-LOG2E * ck_ref[hh, jb])
                kt_ref[hh, jb, HEAD:2 * HEAD, :] = _bias_lanes(row, col_bias, True).astype(BF16)
                return carry
            lax.fori_loop(0, nkb, fill, 0)

    lane = lax.broadcasted_iota(jnp.int32, (tq, HEAD), 1)
    qpos = past + i * tq + lax.broadcasted_iota(jnp.int32, (tq, 1), 0)
    kiota = lax.broadcasted_iota(jnp.int32, (1, tk), 1)
    cq_tile = cq_ref[...]
    pair = lax.broadcasted_iota(jnp.int32, cq_tile.shape, 1)
    q_aug = []
    for hh in range(hps):
        bh = b * heads + hp * hps + hh
        cq = LOG2E * jnp.sum(jnp.where(pair == bh, cq_tile, 0.0), axis=1, keepdims=True)
        aug = _bias_lanes(lane, _split3(cq), False).astype(BF16)
        q_aug.append(jnp.concatenate([q_ref[:, hh * HEAD:(hh + 1) * HEAD], aug], axis=1))

    def make_body(masked):
        def body(j, carry):
            rows = pl.ds(pl.multiple_of(j * tk, tk), tk)

            def causal(s, r0, strip, c0, cols):
                kpos = j * tk + kiota[:, c0:c0 + cols]
                return jnp.where(kpos <= qpos[r0:r0 + strip], s, -jnp.inf)

            for hh in range(hps):
                _online_softmax_block(hh, q_aug[hh], kt_ref[hh, j],
                                      vb_ref[rows, 2 * hh * HEAD:(2 * hh + 2) * HEAD], refs,
                                      causal if masked else None, masked and aligned)
            return carry
        return body

    aligned = tq == tk and past % tk == 0

    n_before, nk = _block_counts(past, i, tq, tk, k_valid)
    _softmax_init(refs)
    _loop_two_per_trip(0, n_before, make_body(False))
    lax.fori_loop(n_before, nk, make_body(True), 0)
    for hh in range(hps):
        o_ref[:, hh * HEAD:(hh + 1) * HEAD] = _softmax_result(refs, hh).astype(o_ref.dtype)


def _fox(zb, k, v, c_rows, c_cols, *, batch, heads, tq_total, tk_total, past, tq, tk):
    _, n, d = zb.shape
    hps = HEADS_PER_STEP
    nq = tq_total // tq
    nkb = tk_total // tk
    kv_spec = pl.BlockSpec((tk_total, hps * HEAD), lambda b, h, i: (b, h))
    return pl.pallas_call(
        functools.partial(_fox_kernel, heads=heads, past=past, k_valid=past + tq_total, tk=tk),
        grid=(batch, heads // hps, nq),
        in_specs=[
            pl.BlockSpec((None, tq, hps * HEAD), lambda b, h, i: (ZB_FQ, b * nq + i, h)),
            kv_spec, kv_spec,
            pl.BlockSpec((tq, batch * heads), lambda b, h, i: (past // tq + i, 0)),
            pl.BlockSpec((hps, nkb, 1, tk), lambda b, h, i: (b * (heads // hps) + h, 0, 0, 0)),
        ],
        out_specs=pl.BlockSpec((tq, hps * HEAD), lambda b, h, i: (b * nq + i, h)),
        out_shape=jax.ShapeDtypeStruct((n, d), BF16),
        scratch_shapes=[pltpu.VMEM((hps, nkb, 2 * HEAD, tk), BF16),
                        pltpu.VMEM((tk_total, 2 * hps * HEAD), BF16)] + _softmax_scratch(hps, tq, tk),
        compiler_params=_params("parallel", "parallel", "arbitrary"),
        name="fox_attn",
    )(zb, k, v, c_rows, c_cols)


def _diff_kernel(q_ref, k_ref, v_ref, slope_ref, lq1_ref, lk1_ref, lq2_ref, lk2_ref, gn_ref,
                 o_ref, kt_ref, vb_ref, *refs, past, k_valid, tk, lam_init):
    i = pl.program_id(2)
    tq = q_ref.shape[0]
    hps = q_ref.shape[1] // HEAD
    nkb = kt_ref.shape[2]
    sub = tk // HEAD
    dq = HEAD // 2

    @pl.when(i == 0)
    def _():
        _values_with_ones(v_ref, vb_ref)
        row = lax.broadcasted_iota(jnp.int32, (HEAD, HEAD), 0)
        col = lax.broadcasted_iota(jnp.int32, (1, HEAD), 1)
        for hh in range(hps):
            slope2 = LOG2E * slope_ref[hh]

            def fill(jb, carry, hh=hh, slope2=slope2):
                r0 = pl.multiple_of(jb * tk, tk)
                for u in range(sub):
                    kblk = k_ref[pl.ds(r0 + u * HEAD, HEAD), hh * HEAD:(hh + 1) * HEAD]
                    kt = kblk.T
                    kpos = (r0 + u * HEAD + col).astype(F32)
                    col_bias = _split3(slope2 * kpos)
                    lanes = slice(u * HEAD, (u + 1) * HEAD)
                    kt_ref[hh, 0, jb, :, lanes] = jnp.where(
                        row < dq, kt, _bias_lanes(row - dq, col_bias, True)).astype(BF16)
                    kt_ref[hh, 1, jb, :, lanes] = jnp.where(
                        row >= dq, kt, _bias_lanes(row, col_bias, True)).astype(BF16)
                return carry
            lax.fori_loop(0, nkb, fill, 0)

    lam = (jnp.exp(jnp.sum(lq1_ref[...] * lk1_ref[...], axis=1, keepdims=True))
           - jnp.exp(jnp.sum(lq2_ref[...] * lk2_ref[...], axis=1, keepdims=True)) + lam_init)

    lane = lax.broadcasted_iota(jnp.int32, (tq, HEAD), 1)
    qpos = past + i * tq + lax.broadcasted_iota(jnp.int32, (tq, 1), 0)
    qposf = qpos.astype(F32)
    qchunk = lax.shift_right_logical(qpos, CHUNK_SHIFT)
    kiota = lax.broadcasted_iota(jnp.int32, (1, tk), 1)
    q_aug, slope_col = [], []
    for hh in range(hps):
        slope2 = LOG2E * slope_ref[hh][:, 0:1]
        slope_col.append(slope2)
        row_bias = _split3(-slope2 * qposf)
        q = q_ref[:, hh * HEAD:(hh + 1) * HEAD].astype(F32)
        q1 = jnp.where(lane < dq, q, _bias_lanes(lane - dq, row_bias, False)).astype(BF16)
        q2 = jnp.where(lane >= dq, q, _bias_lanes(lane, row_bias, False)).astype(BF16)
        q_aug.append((q1, q2))

    def make_body(diagonal):
        def body(j, carry):
            rows = pl.ds(pl.multiple_of(j * tk, tk), tk)
            for hh in range(hps):
                vblk = vb_ref[rows, 2 * hh * HEAD:(2 * hh + 2) * HEAD]

                def fix(s, r0, strip, c0, cols, hh=hh):
                    kpos = j * tk + kiota[:, c0:c0 + cols]
                    ahead = jnp.maximum(kpos - qpos[r0:r0 + strip], 0).astype(F32)
                    visible = jnp.logical_and(
                        lax.shift_right_logical(kpos, CHUNK_SHIFT) <= qchunk[r0:r0 + strip],
                        kpos < k_valid)
                    return jnp.where(visible, s - (2.0 * slope_col[hh]) * ahead, -jnp.inf)

                for mp in range(2):
                    _online_softmax_block(2 * hh + mp, q_aug[hh][mp], kt_ref[hh, mp, j], vblk, refs,
                                          fix if diagonal else None, diagonal and aligned)
            return carry
        return body

    aligned = tq == tk and past % tk == 0

    n_before, nk = _block_counts(past, i, tq, tk, k_valid)
    _softmax_init(refs)
    _loop_two_per_trip(0, n_before, make_body(False))
    lax.fori_loop(n_before, nk, make_body(True), 0)
    for hh in range(hps):
        o = _softmax_result(refs, 2 * hh) - lam * _softmax_result(refs, 2 * hh + 1)
        o_ref[:, hh * HEAD:(hh + 1) * HEAD] = (
            _rms(o, gn_ref[...]) * (1.0 - lam_init)).astype(o_ref.dtype)


def _diff(zb, k, v, slopes, lq1, lk1, lq2, lk2, gn, *, batch, heads, tq_total, tk_total, past,
          tq, tk, lam_init):
    _, n, d = zb.shape
    hps = HEADS_PER_STEP
    nq = tq_total // tq
    nkb = tk_total // tk
    kv_spec = pl.BlockSpec((tk_total, hps * HEAD), lambda b, h, i: (b, h))
    lam_spec = _resident((1, HEAD // 2))
    return pl.pallas_call(
        functools.partial(_diff_kernel, past=past, k_valid=past + tq_total, tk=tk, lam_init=lam_init),
        grid=(batch, heads // hps, nq),
        in_specs=[
            pl.BlockSpec((None, tq, hps * HEAD), lambda b, h, i: (ZB_DQ, b * nq + i, h)),
            kv_spec, kv_spec,
            pl.BlockSpec((hps, 1, HEAD), lambda b, h, i: (h, 0, 0)),
            lam_spec, lam_spec, lam_spec, lam_spec,
            _resident((1, HEAD)),
        ],
        out_specs=pl.BlockSpec((tq, hps * HEAD), lambda b, h, i: (b * nq + i, h)),
        out_shape=jax.ShapeDtypeStruct((n, d), BF16),
        scratch_shapes=[pltpu.VMEM((hps, 2, nkb, HEAD, tk), BF16),
                        pltpu.VMEM((tk_total, 2 * hps * HEAD), BF16)] + _softmax_scratch(2 * hps, tq, tk),
        compiler_params=_params("parallel", "parallel", "arbitrary"),
        name="diff_attn",
    )(zb, k, v, slopes, lq1, lk1, lq2, lk2, gn)


def _merge_kernel(a1_ref, a2_ref, a3_ref, g1_ref, g2_ref, g3_ref, x_ref,
                  w1_ref, w2_ref, w3_ref, wo_ref, o_ref):
    def branch(a_ref, g_ref, w_ref):
        return _sigmoid(g_ref[...].astype(F32)) * jnp.dot(a_ref[...], w_ref[...],
                                                          preferred_element_type=F32)

    mix = branch(a1_ref, g1_ref, w1_ref) + branch(a2_ref, g2_ref, w2_ref) + branch(a3_ref, g3_ref, w3_ref)
    y = jnp.dot(mix.astype(BF16), wo_ref[...], preferred_element_type=F32)
    o_ref[...] = x_ref[...] + y


def _merge(a1, a2, a3, zb, x, w1, w2, w3, wo, *, tm):
    n, d = x.shape
    act = pl.BlockSpec((tm, d), lambda i: (i, 0))

    def gate(slot):
        return pl.BlockSpec((None, tm, d), lambda i: (slot, i, 0))

    wspec = _resident((d, d))
    return pl.pallas_call(
        _merge_kernel,
        grid=(n // tm,),
        in_specs=[act, act, act, gate(ZB_GHG), gate(ZB_GFOX), gate(ZB_GDIFF), act,
                  wspec, wspec, wspec, wspec],
        out_specs=act,
        out_shape=jax.ShapeDtypeStruct((n, d), F32),
        compiler_params=_params("parallel"),
        name="merge_out",
    )(a1, a2, a3, zb, zb, zb, x, w1, w2, w3, wo)


def _ffn_kernel(x_ref, g_ref, wgu_ref, wd_ref, o_ref, *, d_ff, fchunk):
    x = x_ref[...]
    h = _rms(x, g_ref[...]).astype(BF16)
    acc = x
    for c0 in range(0, d_ff, fchunk):
        a = jnp.dot(h, wgu_ref[:, c0:c0 + fchunk], preferred_element_type=F32)
        b = jnp.dot(h, wgu_ref[:, d_ff + c0:d_ff + c0 + fchunk], preferred_element_type=F32)
        act = (a * _sigmoid(a) * b).astype(BF16)
        acc = acc + jnp.dot(act, wd_ref[c0:c0 + fchunk, :], preferred_element_type=F32)
    o_ref[...] = acc


def _ffn(x, g, wgu, wd, *, tm):
    n, d = x.shape
    d_ff = wd.shape[0]
    fchunk = d_ff // 2
    assert fchunk % HEAD == 0
    act = pl.BlockSpec((tm, d), lambda i: (i, 0))
    return pl.pallas_call(
        functools.partial(_ffn_kernel, d_ff=d_ff, fchunk=fchunk),
        grid=(n // tm,),
        in_specs=[act, _resident((1, d)), _resident(wgu.shape), _resident(wd.shape)],
        out_specs=act,
        out_shape=jax.ShapeDtypeStruct((n, d), F32),
        compiler_params=_params("parallel"),
        name="ffn",
    )(x, g, wgu, wd)


def _final_norm_kernel(x_ref, g_ref, o_ref):
    o_ref[...] = _rms(x_ref[...], g_ref[...])


def _final_norm(x, g, *, tm):
    n, d = x.shape
    act = pl.BlockSpec((tm, d), lambda i: (i, 0))
    return pl.pallas_call(
        _final_norm_kernel,
        grid=(n // tm,),
        in_specs=[act, _resident((1, d))],
        out_specs=act,
        out_shape=jax.ShapeDtypeStruct((n, d), F32),
        compiler_params=_params("parallel"),
        name="final_norm",
    )(x, g)


def _prep_layer_weights(w_in_l, d):
    hd = HEADS
    splits = (d, d, d, d, d, d, d, hd, d, d, d, d, d, d)
    offs = [0]
    for s in splits:
        offs.append(offs[-1] + s)
    cols = [w_in_l[:, offs[k]:offs[k + 1]] for k in range(len(splits))]
    hq, hf, hi, hog, fq, fk, fv, ff, dq, dk, dv, g_hg, g_fox, g_diff = cols
    fq = fq * (HEAD ** -0.5 * LOG2E)
    dq = dq * ((HEAD // 2) ** -0.5 * LOG2E)
    order = [hq, hf, hi, hog, fq, dq, g_hg, g_fox, g_diff, fk, fv, dk, dv]
    w = jnp.concatenate(order, axis=1).astype(BF16)
    wff = jnp.pad(ff, ((0, 0), (0, HEAD - hd))).astype(BF16)
    return w, wff


def _pick_tile(n, want):
    t = min(n, want)
    assert n % t == 0
    return t


def _trunk(x, past, wts, lb_all, slopes):
    batch, seq, d = x.shape
    n = batch * seq
    heads = d // HEAD
    bh = batch * heads
    depth = len(wts)
    prompt = past is None
    plen = 0 if prompt else past[0].shape[2]
    tm = _pick_tile(n, 512)

    if prompt:
        chunk, tt = CHUNK, _pick_tile(seq, 512)
        tq = tk = _pick_tile(seq, 512)
        tk_total = seq
    else:
        chunk, tt = seq, seq
        tq, tk = seq, 128
        tk_total = -(-(plen + seq) // tk) * tk
    assert plen % tq == 0
    pad = tk_total - (plen + seq)

    xf = x.reshape(n, d)
    states = []
    for l in range(depth):
        w = wts[l]
        lam_init = 0.8 - 0.6 * math.exp(-0.3 * l)
        zf, zb, fk, fv, dk, dv, fl = _proj_in(xf, w["norm_mix"], w["w_in"], w["w_ff"], w["f_bias"], tm=tm)
        fox_logf = fl[:, :heads].reshape(batch, seq, heads)

        if prompt:
            s0 = jnp.zeros((batch, heads, HEAD, HEAD), F32)
        else:
            s0 = past[5][l]
        hg_o, hg_s = _hgrn(zf, zb, lb_all[l][None, :], w["hg_norm"], s0,
                           batch=batch, seq=seq, chunk=chunk, tt=tt)

        if prompt:
            k_all, v_all, dk_all, dv_all, logf_all = fk, fv, dk, dv, fox_logf
        else:
            def with_past(cache, new):
                new4 = new.reshape(batch, seq, heads, HEAD)
                full = jnp.concatenate([cache, new4], axis=1)
                full = jnp.pad(full, ((0, 0), (0, pad), (0, 0), (0, 0)))
                return full.reshape(batch * tk_total, d)
            k_all, v_all = with_past(past[0][l], fk), with_past(past[1][l], fv)
            dk_all, dv_all = with_past(past[3][l], dk), with_past(past[4][l], dv)
            logf_all = jnp.pad(jnp.concatenate([past[2][l], fox_logf], axis=1),
                               ((0, 0), (0, pad), (0, 0)))
        c_rows = _cumsum_time(logf_all.transpose(1, 0, 2).reshape(tk_total, bh))
        c_cols = c_rows.T.reshape(bh, tk_total // tk, 1, tk)
        fox_o = _fox(zb, k_all, v_all, c_rows, c_cols, batch=batch, heads=heads,
                     tq_total=seq, tk_total=tk_total, past=plen, tq=tq, tk=tk)

        diff_o = _diff(zb, dk_all, dv_all, slopes, w["lq1"], w["lk1"], w["lq2"], w["lk2"],
                       w["diff_norm"], batch=batch, heads=heads, tq_total=seq, tk_total=tk_total,
                       past=plen, tq=tq, tk=tk, lam_init=lam_init)

        xf = _merge(hg_o, fox_o, diff_o, zb, xf, w["w_br_hgrn"], w["w_br_fox"], w["w_br_diff"],
                    w["w_out"], tm=tm)
        xf = _ffn(xf, w["norm_ffn"], w["w_gate_up"], w["w_down"], tm=tm)

        def heads4(a):
            return a.reshape(batch, seq, heads, HEAD)
        states.append((heads4(fk), heads4(fv), fox_logf, heads4(dk), heads4(dv), hg_s))

    y = _final_norm(xf, wts[0]["norm_final"], tm=tm).reshape(batch, seq, d)
    stacked = tuple(jnp.stack(s, axis=0) for s in zip(*states))
    return y, stacked


def kernel(x_prompt, x_sample, cache_fox_k, cache_fox_v, cache_fox_logf, cache_diff_k, cache_diff_v,
           state_hgrn, norm_mix, w_in, hg_lb_logits, hg_norm, fox_f_bias, diff_lam_q1, diff_lam_k1,
           diff_lam_q2, diff_lam_k2, diff_norm, w_br_hgrn, w_br_fox, w_br_diff, w_out, norm_ffn,
           w_gate_up, w_down, norm_final):
    depth, d, _ = w_in.shape
    wts = []
    for l in range(depth):
        w, wff = _prep_layer_weights(w_in[l], d)
        wts.append(dict(
            norm_mix=norm_mix[l][None, :], w_in=w, w_ff=wff,
            f_bias=jnp.pad(fox_f_bias[l], (0, HEAD - HEADS))[None, :],
            hg_norm=hg_norm[l][None, :],
            lq1=diff_lam_q1[l][None, :], lk1=diff_lam_k1[l][None, :],
            lq2=diff_lam_q2[l][None, :], lk2=diff_lam_k2[l][None, :],
            diff_norm=diff_norm[l][None, :],
            w_br_hgrn=w_br_hgrn[l].astype(BF16), w_br_fox=w_br_fox[l].astype(BF16),
            w_br_diff=w_br_diff[l].astype(BF16), w_out=w_out[l].astype(BF16),
            norm_ffn=norm_ffn[l][None, :],
            w_gate_up=w_gate_up[l].astype(BF16), w_down=w_down[l].astype(BF16),
            norm_final=norm_final[None, :],
        ))
    cs = jnp.cumsum(jax.nn.softmax(hg_lb_logits.astype(F32), axis=0), axis=0)
    lb_all = cs - cs[0:1]
    slope = jnp.exp2(-8.0 * jnp.arange(1, HEADS + 1, dtype=F32) / HEADS)
    slopes = jnp.broadcast_to(slope[:, None, None], (HEADS, 1, HEAD))

    y_p, st_p = _trunk(x_prompt, None, wts, lb_all, slopes)
    past = (cache_fox_k, cache_fox_v, cache_fox_logf, cache_diff_k, cache_diff_v, state_hgrn)
    y_s, st_s = _trunk(x_sample, past, wts, lb_all, slopes)
    return (y_p, y_s) + st_p + st_s
```

```python
import functools
import math

import jax
import jax.numpy as jnp
from jax import lax
from jax.experimental import pallas as pl
from jax.experimental.pallas import tpu as pltpu

EPS = 1e-6
CHUNK = 64
CHUNK_SHIFT = 6
assert 1 << CHUNK_SHIFT == CHUNK
HEAD = 128
HEADS = 8
HEADS_PER_STEP = 2
LOG2E = math.log2(math.e)
UNDERFLOW_LOG2 = 160.0

ZF_HQ, ZF_HF = range(2)
ZB_HI, ZB_HOG, ZB_FQ, ZB_DQ, ZB_GHG, ZB_GFOX, ZB_GDIFF = range(7)
ZS_FK, ZS_FV, ZS_DK, ZS_DV = range(4)
N_ZF, N_ZB, N_ZS = 2, 7, 4

V7X_VMEM_LIMIT_BYTES = 56 * 1024 * 1024

_NT = (((1,), (1,)), ((), ()))
_TN = (((0,), (0,)), ((), ()))
BF16 = jnp.bfloat16
F32 = jnp.float32


def _params(*semantics):
    return pltpu.CompilerParams(dimension_semantics=semantics,
                                vmem_limit_bytes=V7X_VMEM_LIMIT_BYTES)


def _sigmoid(x):
    return 1.0 / (1.0 + jnp.exp(-x))


def _log1p_exp_neg(a):
    return jnp.log(1.0 + jnp.exp(-a))


def _log_sigmoid(x):
    return jnp.minimum(x, 0.0) - _log1p_exp_neg(jnp.abs(x))


def _rms(x, g):
    var = jnp.mean(x * x, axis=-1, keepdims=True)
    return (x * lax.rsqrt(var + EPS)) * g


def _resident(shape):
    nd = len(shape)
    return pl.BlockSpec(shape, lambda *_: (0,) * nd)


def _split3(x):
    hi = x.astype(BF16).astype(F32)
    r = x - hi
    mid = r.astype(BF16).astype(F32)
    lo = (r - mid).astype(BF16).astype(F32)
    return hi, mid, lo


def _bias_lanes(pos, row_terms, one_lanes_first):
    t0, t1, t2 = row_terms
    base = 3 if one_lanes_first else 0
    ones_lo = 0 if one_lanes_first else 3
    out = jnp.where(pos == base, t0, jnp.where(pos == base + 1, t1, jnp.where(pos == base + 2, t2, 0.0)))
    return jnp.where(jnp.logical_and(pos >= ones_lo, pos < ones_lo + 3), 1.0, out)


def _proj_in_kernel(x_ref, g_ref, w_ref, wff_ref, fb_ref, zf_ref, zb_ref, zs_ref, fl_ref, h_ref):
    j = pl.program_id(1)

    @pl.when(j == 0)
    def _():
        h = _rms(x_ref[...], g_ref[...]).astype(BF16)
        h_ref[...] = h
        ff = jnp.dot(h, wff_ref[...], preferred_element_type=F32)
        fl_ref[...] = _log_sigmoid(ff + fb_ref[...])

    res = jnp.dot(h_ref[...], w_ref[...], preferred_element_type=F32)

    @pl.when(j < N_ZF)
    def _():
        zf_ref[...] = res

    @pl.when(jnp.logical_and(j >= N_ZF, j < N_ZF + N_ZB))
    def _():
        zb_ref[...] = res.astype(BF16)

    @pl.when(j >= N_ZF + N_ZB)
    def _():
        zs_ref[...] = res


def _proj_in(x, g, w, wff, fb, *, tm):
    n, d = x.shape
    ngroups = N_ZF + N_ZB + N_ZS
    return pl.pallas_call(
        _proj_in_kernel,
        grid=(n // tm, ngroups),
        in_specs=[
            pl.BlockSpec((tm, d), lambda i, j: (i, 0)),
            _resident((1, d)),
            pl.BlockSpec((d, d), lambda i, j: (0, j)),
            _resident((d, HEAD)),
            _resident((1, HEAD)),
        ],
        out_specs=[
            pl.BlockSpec((None, tm, d), lambda i, j: (jnp.minimum(j, N_ZF - 1), i, 0)),
            pl.BlockSpec((None, tm, d), lambda i, j: (jnp.clip(j - N_ZF, 0, N_ZB - 1), i, 0)),
            pl.BlockSpec((None, tm, d),
                         lambda i, j: (jnp.clip(j - N_ZF - N_ZB, 0, N_ZS - 1), i, 0)),
            pl.BlockSpec((tm, HEAD), lambda i, j: (i, 0)),
        ],
        out_shape=[
            jax.ShapeDtypeStruct((N_ZF, n, d), F32),
            jax.ShapeDtypeStruct((N_ZB, n, d), BF16),
            jax.ShapeDtypeStruct((N_ZS, n, d), F32),
            jax.ShapeDtypeStruct((n, HEAD), F32),
        ],
        scratch_shapes=[pltpu.VMEM((tm, d), BF16)],
        compiler_params=_params("parallel", "arbitrary"),
        name="proj_in",
    )(x, g, w, wff, fb)


def _hgrn_kernel(zq_ref, zf_ref, zi_ref, zog_ref, lb_ref, gn_ref, s0_ref,
                 o_ref, sout_ref, st_ref, *, chunk, heads):
    t = pl.program_id(1)
    tt = zq_ref.shape[0]

    @pl.when(t == 0)
    def _():
        for h in range(heads):
            st_ref[h] = s0_ref[0, h].T

    lb = lb_ref[...]
    log_lb = jnp.log(lb)
    log_1m_lb = jnp.log1p(-lb)
    gn = gn_ref[...]
    row = lax.broadcasted_iota(jnp.int32, (chunk, chunk), 0)
    col = lax.broadcasted_iota(jnp.int32, (chunk, chunk), 1)
    causal = row >= col
    tril = causal.astype(F32)
    mid = chunk // 2

    def one_chunk(c, carry):
        r0 = pl.multiple_of(c * chunk, chunk)
        rows = pl.ds(r0, chunk)
        zf = zf_ref[rows, :]
        b = log_1m_lb + _log_sigmoid(zf)
        logf = jnp.maximum(log_lb, b) + _log1p_exp_neg(jnp.abs(log_lb - b))
        kk = (1.0 - lb) * _sigmoid(-zf)
        L = jnp.dot(tril, logf, preferred_element_type=F32, precision=lax.Precision.HIGHEST)
        for h in range(heads):
            sl = slice(h * HEAD, (h + 1) * HEAD)
            Lh = L[:, sl]
            Lr = Lh[mid:mid + 1, :]
            Lend = Lh[chunk - 1:chunk, :]
            q = zq_ref[rows, sl]
            k = kk[:, sl]
            v = zi_ref[rows, sl]
            q_in = (q * jnp.exp(Lh)).astype(BF16)
            q_r = (q * jnp.exp(Lh - Lr)).astype(BF16)
            k_r = (k * jnp.exp(Lr - Lh)).astype(BF16)
            k_end = (k * jnp.exp(Lend - Lh)).astype(BF16)
            a = lax.dot_general(q_r, k_r, _NT, preferred_element_type=F32)
            a = jnp.where(causal, a, 0.0).astype(BF16)
            st = st_ref[h]
            o = (lax.dot_general(q_in, st.astype(BF16), _NT, preferred_element_type=F32)
                 + jnp.dot(a, v, preferred_element_type=F32))
            st_ref[h] = st * jnp.exp(Lend) + lax.dot_general(v, k_end, _TN, preferred_element_type=F32)
            og = zog_ref[rows, sl].astype(F32)
            o_ref[rows, sl] = (_rms(o, gn) * (og * _sigmoid(og))).astype(o_ref.dtype)
        return carry

    nchunks = tt // chunk
    lax.fori_loop(0, nchunks, one_chunk, 0, unroll=2 if nchunks % 2 == 0 else 1)

    @pl.when(t == pl.num_programs(1) - 1)
    def _():
        for h in range(heads):
            sout_ref[0, h] = st_ref[h].T


def _hgrn(zf, zb, lb, gn, s0, *, batch, seq, chunk, tt):
    _, n, d = zf.shape
    heads = d // HEAD
    nt = seq // tt

    def zspec(slot):
        return pl.BlockSpec((None, tt, d), lambda b, t: (slot, b * nt + t, 0))

    state_spec = pl.BlockSpec((1, heads, HEAD, HEAD), lambda b, t: (b, 0, 0, 0))
    return pl.pallas_call(
        functools.partial(_hgrn_kernel, chunk=chunk, heads=heads),
        grid=(batch, nt),
        in_specs=[zspec(ZF_HQ), zspec(ZF_HF), zspec(ZB_HI), zspec(ZB_HOG),
                  _resident((1, d)), _resident((1, HEAD)), state_spec],
        out_specs=[pl.BlockSpec((tt, d), lambda b, t: (b * nt + t, 0)), state_spec],
        out_shape=[jax.ShapeDtypeStruct((n, d), BF16),
                   jax.ShapeDtypeStruct((batch, heads, HEAD, HEAD), F32)],
        scratch_shapes=[pltpu.VMEM((heads, HEAD, HEAD), F32)],
        compiler_params=_params("parallel", "arbitrary"),
        name="hgrn",
    )(zf, zf, zb, zb, lb, gn, s0)


def _cumsum_kernel(x_ref, o_ref, *, blk):
    nblk = x_ref.shape[0] // blk
    row = lax.broadcasted_iota(jnp.int32, (blk, blk), 0)
    col = lax.broadcasted_iota(jnp.int32, (blk, blk), 1)
    tril = (row >= col).astype(F32)

    def body(i, carry):
        rows = pl.ds(pl.multiple_of(i * blk, blk), blk)
        c = jnp.dot(tril, x_ref[rows, :], preferred_element_type=F32,
                    precision=lax.Precision.HIGHEST) + carry
        o_ref[rows, :] = c
        return c[blk - 1:blk, :]

    lax.fori_loop(0, nblk, body, jnp.zeros((1, x_ref.shape[1]), F32))


def _cumsum_time(x):
    return pl.pallas_call(
        functools.partial(_cumsum_kernel, blk=128),
        out_shape=jax.ShapeDtypeStruct(x.shape, F32),
        compiler_params=_params(),
        name="fox_cumsum",
    )(x)


STRIP = 64


def _lane_tiles(x, n):
    return x if n == 1 else jnp.concatenate([x] * n, axis=1)


def _online_softmax_block(c, q_aug, kt_blk, vblk, refs, fix_scores=None, aligned_diagonal=False):
    s_ref, p_ref, m_ref, acc_ref = refs
    tq, tk = s_ref.shape[1], s_ref.shape[2]
    strip = min(tq, STRIP)
    s_ref[c] = jnp.dot(q_aug, kt_blk, preferred_element_type=F32)
    for r0 in range(0, tq, strip):
        rs = slice(r0, r0 + strip)
        if aligned_diagonal:
            lo = r0 // HEAD * HEAD
            hi = (r0 + strip - 1) // HEAD * HEAD + HEAD
        else:
            lo, hi = 0, tk
        s = s_ref[c, rs, :hi]
        if fix_scores is not None:
            fixed = fix_scores(s[:, lo:], r0, strip, lo, hi - lo)
            s = fixed if lo == 0 else jnp.concatenate([s[:, :lo], fixed], axis=1)
        m_prev = m_ref[c, rs, :]
        m_new = jnp.maximum(m_prev, jnp.max(s, axis=1, keepdims=True))
        alpha = jnp.exp2(m_prev - m_new)
        p = jnp.exp2(s - _lane_tiles(m_new, hi // HEAD))
        m_ref[c, rs, :] = m_new
        acc_ref[c, rs, :] = _lane_tiles(alpha, 2) * acc_ref[c, rs, :]
        p_ref[c, rs, :hi] = p.astype(BF16)
        if hi < tk:
            p_ref[c, rs, hi:] = jnp.zeros((strip, tk - hi), BF16)
    acc_ref[c] += jnp.dot(p_ref[c], vblk, preferred_element_type=F32)


def _softmax_init(refs):
    _, _, m_ref, acc_ref = refs
    m_ref[...] = jnp.full(m_ref.shape, -jnp.inf, F32)
    acc_ref[...] = jnp.zeros(acc_ref.shape, F32)


def _softmax_result(refs, c):
    acc_ref = refs[3]
    return acc_ref[c, :, :HEAD] / acc_ref[c, :, HEAD:]


def _values_with_ones(v_ref, vb_ref):
    rows = v_ref.shape[0]
    for hh in range(v_ref.shape[1] // HEAD):
        vb_ref[:, 2 * hh * HEAD:(2 * hh + 1) * HEAD] = v_ref[:, hh * HEAD:(hh + 1) * HEAD].astype(BF16)
        vb_ref[:, (2 * hh + 1) * HEAD:(2 * hh + 2) * HEAD] = jnp.ones((rows, HEAD), BF16)


def _softmax_scratch(chains, tq, tk):
    return [pltpu.VMEM((chains, tq, tk), F32), pltpu.VMEM((chains, tq, tk), BF16),
            pltpu.VMEM((chains, tq, HEAD), F32), pltpu.VMEM((chains, tq, 2 * HEAD), F32)]


def _loop_two_per_trip(lo, hi, body):
    pairs = (hi - lo) // 2

    def two(t, carry):
        j = lo + 2 * t
        body(j, 0)
        body(j + 1, 0)
        return carry

    lax.fori_loop(0, pairs, two, 0)
    tail = lo + 2 * pairs

    @pl.when(tail < hi)
    def _():
        body(tail, 0)


def _block_counts(past, i, tq, tk, k_valid):
    n_before = (past + i * tq) // tk
    kmax = jnp.minimum(((past + (i + 1) * tq - 1) // CHUNK + 1) * CHUNK, k_valid)
    return n_before, (kmax + tk - 1) // tk


def _fox_kernel(q_ref, k_ref, v_ref, cq_ref, ck_ref, o_ref, kt_ref, vb_ref, stats_ref, *refs,
                heads, past, k_valid, tk):
    b, hp, i = pl.program_id(0), pl.program_id(1), pl.program_id(2)
    tq = q_ref.shape[0]
    hps = q_ref.shape[1] // HEAD
    nkb = kt_ref.shape[1]
    sub = tk // HEAD

    blk = lax.broadcasted_iota(jnp.int32, (1, HEAD), 1)
    assert nkb <= HEAD

    @pl.when(i == 0)
    def _():
        _values_with_ones(v_ref, vb_ref)
        stats_ref[...] = jnp.zeros(stats_ref.shape, F32)
        row = lax.broadcasted_iota(jnp.int32, (HEAD, tk), 0)
        for hh in range(hps):
            def fill(jb, carry, hh=hh):
                r0 = pl.multiple_of(jb * tk, tk)
                per_key = jnp.zeros((1, HEAD), F32)
                for u in range(sub):
                    kblk = k_ref[pl.ds(r0 + u * HEAD, HEAD), hh * HEAD:(hh + 1) * HEAD]
                    kt = kblk.T.astype(BF16)
                    kt_ref[hh, jb, 0:HEAD, u * HEAD:(u + 1) * HEAD] = kt
                    ktf = kt.astype(F32)
                    per_key = jnp.maximum(per_key, jnp.sum(ktf * ktf, axis=0, keepdims=True))
                norm2 = jnp.max(per_key, axis=1, keepdims=True)
                ck = LOG2E * ck_ref[hh, jb]
                kt_ref[hh, jb, HEAD:2 * HEAD, :] = _bias_lanes(row, _split3(-ck), True).astype(BF16)
                here = blk == jb
                stats_ref[hh, 0:1, :] = jnp.where(here, norm2, stats_ref[hh, 0:1, :])
                stats_ref[hh, 1:2, :] = jnp.where(here, jnp.min(ck, axis=1, keepdims=True),
                                                  stats_ref[hh, 1:2, :])
                return carry
            lax.fori_loop(0, nkb, fill, 0)

    def first_needed_block(hh, cq, n_before):
        q32 = q_ref[:, hh * HEAD:(hh + 1) * HEAD].astype(F32)
        nq2 = jnp.max(jnp.sum(q32 * q32, axis=1, keepdims=True), axis=0, keepdims=True)
        reach = jnp.sqrt(nq2 * stats_ref[hh, 0:1, :])
        own = jnp.max(jnp.where(blk == n_before, reach, 0.0), axis=1, keepdims=True)
        upper = reach + jnp.max(cq, axis=0, keepdims=True) - stats_ref[hh, 1:2, :] + own
        needed = jnp.logical_or(jnp.logical_not(upper < -UNDERFLOW_LOG2), blk >= n_before)
        return jnp.min(jnp.where(needed, blk, HEAD))

    lane = lax.broadcasted_iota(jnp.int32, (tq, HEAD), 1)
    qpos = past + i * tq + lax.broadcasted_iota(jnp.int32, (tq, 1), 0)
    kiota = lax.broadcasted_iota(jnp.int32, (1, tk), 1)
    cq_tile = cq_ref[...]
    pair = lax.broadcasted_iota(jnp.int32, cq_tile.shape, 1)
    aligned = tq == tk and past % tk == 0
    n_before, nk = _block_counts(past, i, tq, tk, k_valid)
    q_aug = []
    j_start = n_before if aligned else 0
    for hh in range(hps):
        bh = b * heads + hp * hps + hh
        cq = LOG2E * jnp.sum(jnp.where(pair == bh, cq_tile, 0.0), axis=1, keepdims=True)
        aug = _bias_lanes(lane, _split3(cq), False).astype(BF16)
        q_aug.append(jnp.concatenate([q_ref[:, hh * HEAD:(hh + 1) * HEAD], aug], axis=1))
        if aligned:
            j_start = jnp.minimum(j_start, first_needed_block(hh, cq, n_before))

    def make_body(masked):
        def body(j, carry):
            rows = pl.ds(pl.multiple_of(j * tk, tk), tk)

            def causal(s, r0, strip, c0, cols):
                kpos = j * tk + kiota[:, c0:c0 + cols]
                return jnp.where(kpos <= qpos[r0:r0 + strip], s, -jnp.inf)

            for hh in range(hps):
                _online_softmax_block(hh, q_aug[hh], kt_ref[hh, j],
                                      vb_ref[rows, 2 * hh * HEAD:(2 * hh + 2) * HEAD], refs,
                                      causal if masked else None, masked and aligned)
            return carry
        return body

    _softmax_init(refs)
    _loop_two_per_trip(j_start, n_before, make_body(False))
    lax.fori_loop(n_before, nk, make_body(True), 0)
    for hh in range(hps):
        o_ref[:, hh * HEAD:(hh + 1) * HEAD] = _softmax_result(refs, hh).astype(o_ref.dtype)


def _fox(zb, k, v, c_rows, c_cols, *, batch, heads, tq_total, tk_total, past, tq, tk):
    _, n, d = zb.shape
    hps = HEADS_PER_STEP
    nq = tq_total // tq
    nkb = tk_total // tk
    (k, k_slot), (v, v_slot) = k, v

    def kv_spec(slot):
        return pl.BlockSpec((None, tk_total, hps * HEAD), lambda b, h, i: (slot, b, h))
    return pl.pallas_call(
        functools.partial(_fox_kernel, heads=heads, past=past, k_valid=past + tq_total, tk=tk),
        grid=(batch, heads // hps, nq),
        in_specs=[
            pl.BlockSpec((None, tq, hps * HEAD), lambda b, h, i: (ZB_FQ, b * nq + i, h)),
            kv_spec(k_slot), kv_spec(v_slot),
            pl.BlockSpec((tq, batch * heads), lambda b, h, i: (past // tq + i, 0)),
            pl.BlockSpec((hps, nkb, 1, tk), lambda b, h, i: (b * (heads // hps) + h, 0, 0, 0)),
        ],
        out_specs=pl.BlockSpec((tq, hps * HEAD), lambda b, h, i: (b * nq + i, h)),
        out_shape=jax.ShapeDtypeStruct((n, d), BF16),
        scratch_shapes=[pltpu.VMEM((hps, nkb, 2 * HEAD, tk), BF16),
                        pltpu.VMEM((tk_total, 2 * hps * HEAD), BF16),
                        pltpu.VMEM((hps, 8, HEAD), F32)] + _softmax_scratch(hps, tq, tk),
        compiler_params=_params("parallel", "parallel", "arbitrary"),
        name="fox_attn",
    )(zb, k, v, c_rows, c_cols)


def _diff_kernel(q_ref, k_ref, v_ref, slope_ref, lq1_ref, lk1_ref, lq2_ref, lk2_ref, gn_ref,
                 o_ref, kt_ref, vb_ref, *refs, past, k_valid, tk, lam_init):
    i = pl.program_id(2)
    tq = q_ref.shape[0]
    hps = q_ref.shape[1] // HEAD
    nkb = kt_ref.shape[2]
    sub = tk // HEAD
    dq = HEAD // 2

    @pl.when(i == 0)
    def _():
        _values_with_ones(v_ref, vb_ref)
        row = lax.broadcasted_iota(jnp.int32, (HEAD, HEAD), 0)
        col = lax.broadcasted_iota(jnp.int32, (1, HEAD), 1)
        for hh in range(hps):
            slope2 = LOG2E * slope_ref[hh]

            def fill(jb, carry, hh=hh, slope2=slope2):
                r0 = pl.multiple_of(jb * tk, tk)
                for u in range(sub):
                    kblk = k_ref[pl.ds(r0 + u * HEAD, HEAD), hh * HEAD:(hh + 1) * HEAD]
                    kt = kblk.T
                    kpos = (r0 + u * HEAD + col).astype(F32)
                    col_bias = _split3(slope2 * kpos)
                    lanes = slice(u * HEAD, (u + 1) * HEAD)
                    kt_ref[hh, 0, jb, :, lanes] = jnp.where(
                        row < dq, kt, _bias_lanes(row - dq, col_bias, True)).astype(BF16)
                    kt_ref[hh, 1, jb, :, lanes] = jnp.where(
                        row >= dq, kt, _bias_lanes(row, col_bias, True)).astype(BF16)
                return carry
            lax.fori_loop(0, nkb, fill, 0)

    lam = (jnp.exp(jnp.sum(lq1_ref[...] * lk1_ref[...], axis=1, keepdims=True))
           - jnp.exp(jnp.sum(lq2_ref[...] * lk2_ref[...], axis=1, keepdims=True)) + lam_init)

    lane = lax.broadcasted_iota(jnp.int32, (tq, HEAD), 1)
    qpos = past + i * tq + lax.broadcasted_iota(jnp.int32, (tq, 1), 0)
    qposf = qpos.astype(F32)
    qchunk = lax.shift_right_logical(qpos, CHUNK_SHIFT)
    kiota = lax.broadcasted_iota(jnp.int32, (1, tk), 1)
    q_aug, slope_col = [], []
    for hh in range(hps):
        slope2 = LOG2E * slope_ref[hh][:, 0:1]
        slope_col.append(slope2)
        row_bias = _split3(-slope2 * qposf)
        q = q_ref[:, hh * HEAD:(hh + 1) * HEAD].astype(F32)
        q1 = jnp.where(lane < dq, q, _bias_lanes(lane - dq, row_bias, False)).astype(BF16)
        q2 = jnp.where(lane >= dq, q, _bias_lanes(lane, row_bias, False)).astype(BF16)
        q_aug.append((q1, q2))

    def make_body(diagonal):
        def body(j, carry):
            rows = pl.ds(pl.multiple_of(j * tk, tk), tk)
            for hh in range(hps):
                vblk = vb_ref[rows, 2 * hh * HEAD:(2 * hh + 2) * HEAD]

                def fix(s, r0, strip, c0, cols, hh=hh):
                    kpos = j * tk + kiota[:, c0:c0 + cols]
                    ahead = jnp.maximum(kpos - qpos[r0:r0 + strip], 0).astype(F32)
                    visible = jnp.logical_and(
                        lax.shift_right_logical(kpos, CHUNK_SHIFT) <= qchunk[r0:r0 + strip],
                        kpos < k_valid)
                    return jnp.where(visible, s - (2.0 * slope_col[hh]) * ahead, -jnp.inf)

                for mp in range(2):
                    _online_softmax_block(2 * hh + mp, q_aug[hh][mp], kt_ref[hh, mp, j], vblk, refs,
                                          fix if diagonal else None, diagonal and aligned)
            return carry
        return body

    aligned = tq == tk and past % tk == 0

    n_before, nk = _block_counts(past, i, tq, tk, k_valid)
    _softmax_init(refs)
    _loop_two_per_trip(0, n_before, make_body(False))
    lax.fori_loop(n_before, nk, make_body(True), 0)
    for hh in range(hps):
        o = _softmax_result(refs, 2 * hh) - lam * _softmax_result(refs, 2 * hh + 1)
        o_ref[:, hh * HEAD:(hh + 1) * HEAD] = (
            _rms(o, gn_ref[...]) * (1.0 - lam_init)).astype(o_ref.dtype)


def _diff(zb, k, v, slopes, lq1, lk1, lq2, lk2, gn, *, batch, heads, tq_total, tk_total, past,
          tq, tk, lam_init):
    _, n, d = zb.shape
    hps = HEADS_PER_STEP
    nq = tq_total // tq
    nkb = tk_total // tk
    (k, k_slot), (v, v_slot) = k, v

    def kv_spec(slot):
        return pl.BlockSpec((None, tk_total, hps * HEAD), lambda b, h, i: (slot, b, h))
    lam_spec = _resident((1, HEAD // 2))
    return pl.pallas_call(
        functools.partial(_diff_kernel, past=past, k_valid=past + tq_total, tk=tk, lam_init=lam_init),
        grid=(batch, heads // hps, nq),
        in_specs=[
            pl.BlockSpec((None, tq, hps * HEAD), lambda b, h, i: (ZB_DQ, b * nq + i, h)),
            kv_spec(k_slot), kv_spec(v_slot),
            pl.BlockSpec((hps, 1, HEAD), lambda b, h, i: (h, 0, 0)),
            lam_spec, lam_spec, lam_spec, lam_spec,
            _resident((1, HEAD)),
        ],
        out_specs=pl.BlockSpec((tq, hps * HEAD), lambda b, h, i: (b * nq + i, h)),
        out_shape=jax.ShapeDtypeStruct((n, d), BF16),
        scratch_shapes=[pltpu.VMEM((hps, 2, nkb, HEAD, tk), BF16),
                        pltpu.VMEM((tk_total, 2 * hps * HEAD), BF16)] + _softmax_scratch(2 * hps, tq, tk),
        compiler_params=_params("parallel", "parallel", "arbitrary"),
        name="diff_attn",
    )(zb, k, v, slopes, lq1, lk1, lq2, lk2, gn)


def _merge_kernel(a1_ref, a2_ref, a3_ref, g1_ref, g2_ref, g3_ref, x_ref,
                  w1_ref, w2_ref, w3_ref, wo_ref, o_ref):
    def branch(a_ref, g_ref, w_ref):
        return _sigmoid(g_ref[...].astype(F32)) * jnp.dot(a_ref[...], w_ref[...],
                                                          preferred_element_type=F32)

    mix = branch(a1_ref, g1_ref, w1_ref) + branch(a2_ref, g2_ref, w2_ref) + branch(a3_ref, g3_ref, w3_ref)
    y = jnp.dot(mix.astype(BF16), wo_ref[...], preferred_element_type=F32)
    o_ref[...] = x_ref[...] + y


def _merge(a1, a2, a3, zb, x, w1, w2, w3, wo, *, tm):
    n, d = x.shape
    act = pl.BlockSpec((tm, d), lambda i: (i, 0))

    def gate(slot):
        return pl.BlockSpec((None, tm, d), lambda i: (slot, i, 0))

    wspec = _resident((d, d))
    return pl.pallas_call(
        _merge_kernel,
        grid=(n // tm,),
        in_specs=[act, act, act, gate(ZB_GHG), gate(ZB_GFOX), gate(ZB_GDIFF), act,
                  wspec, wspec, wspec, wspec],
        out_specs=act,
        out_shape=jax.ShapeDtypeStruct((n, d), F32),
        compiler_params=_params("parallel"),
        name="merge_out",
    )(a1, a2, a3, zb, zb, zb, x, w1, w2, w3, wo)


def _ffn_kernel(x_ref, g_ref, wgu_ref, wd_ref, gout_ref, o_ref, *, d_ff, fchunk, norm_out):
    x = x_ref[...]
    h = _rms(x, g_ref[...]).astype(BF16)
    acc = x
    for c0 in range(0, d_ff, fchunk):
        a = jnp.dot(h, wgu_ref[:, c0:c0 + fchunk], preferred_element_type=F32)
        b = jnp.dot(h, wgu_ref[:, d_ff + c0:d_ff + c0 + fchunk], preferred_element_type=F32)
        act = (a * _sigmoid(a) * b).astype(BF16)
        acc = acc + jnp.dot(act, wd_ref[c0:c0 + fchunk, :], preferred_element_type=F32)
    o_ref[...] = _rms(acc, gout_ref[...]) if norm_out else acc


def _ffn(x, g, wgu, wd, g_out, *, tm, norm_out):
    n, d = x.shape
    d_ff = wd.shape[0]
    fchunk = d_ff // 2
    assert fchunk % HEAD == 0
    act = pl.BlockSpec((tm, d), lambda i: (i, 0))
    return pl.pallas_call(
        functools.partial(_ffn_kernel, d_ff=d_ff, fchunk=fchunk, norm_out=norm_out),
        grid=(n // tm,),
        in_specs=[act, _resident((1, d)), _resident(wgu.shape), _resident(wd.shape), _resident((1, d))],
        out_specs=act,
        out_shape=jax.ShapeDtypeStruct((n, d), F32),
        compiler_params=_params("parallel"),
        name="ffn",
    )(x, g, wgu, wd, g_out)


def _prep_layer_weights(w_in_l, d):
    hd = HEADS
    splits = (d, d, d, d, d, d, d, hd, d, d, d, d, d, d)
    offs = [0]
    for s in splits:
        offs.append(offs[-1] + s)
    cols = [w_in_l[:, offs[k]:offs[k + 1]] for k in range(len(splits))]
    hq, hf, hi, hog, fq, fk, fv, ff, dq, dk, dv, g_hg, g_fox, g_diff = cols
    fq = fq * (HEAD ** -0.5 * LOG2E)
    dq = dq * ((HEAD // 2) ** -0.5 * LOG2E)
    order = [hq, hf, hi, hog, fq, dq, g_hg, g_fox, g_diff, fk, fv, dk, dv]
    w = jnp.concatenate(order, axis=1).astype(BF16)
    wff = jnp.pad(ff, ((0, 0), (0, HEAD - hd))).astype(BF16)
    return w, wff


def _pick_tile(n, want):
    t = min(n, want)
    assert n % t == 0
    return t


def _trunk(x, past, wts, lb_all, slopes):
    batch, seq, d = x.shape
    n = batch * seq
    heads = d // HEAD
    bh = batch * heads
    depth = len(wts)
    prompt = past is None
    plen = 0 if prompt else past[0].shape[2]
    tm = _pick_tile(n, 512)
    tm_proj = _pick_tile(n, 1024)

    if prompt:
        chunk, tt = CHUNK, _pick_tile(seq, 512)
        tq = tk = _pick_tile(seq, 512)
        tk_total = seq
    else:
        chunk, tt = seq, seq
        tq = seq
        tk = tk_total = -(-(plen + seq) // HEAD) * HEAD
    assert plen % tq == 0
    pad = tk_total - (plen + seq)

    xf = x.reshape(n, d)
    states = []
    for l in range(depth):
        w = wts[l]
        lam_init = 0.8 - 0.6 * math.exp(-0.3 * l)
        zf, zb, zs, fl = _proj_in(xf, w["norm_mix"], w["w_in"], w["w_ff"], w["f_bias"], tm=tm_proj)
        fk, fv, dk, dv = zs[ZS_FK], zs[ZS_FV], zs[ZS_DK], zs[ZS_DV]
        fox_logf = fl[:, :heads].reshape(batch, seq, heads)

        if prompt:
            s0 = jnp.zeros((batch, heads, HEAD, HEAD), F32)
        else:
            s0 = past[5][l]
        hg_o, hg_s = _hgrn(zf, zb, lb_all[l][None, :], w["hg_norm"], s0,
                           batch=batch, seq=seq, chunk=chunk, tt=tt)

        if prompt:
            fox_k, fox_v = (zs, ZS_FK), (zs, ZS_FV)
            diff_k, diff_v = (zs, ZS_DK), (zs, ZS_DV)
            logf_all = fox_logf
        else:
            def with_past(cache, new):
                new4 = new.reshape(batch, seq, heads, HEAD)
                full = jnp.concatenate([cache, new4], axis=1)
                full = jnp.pad(full, ((0, 0), (0, pad), (0, 0), (0, 0)))
                return full.reshape(1, batch * tk_total, d), 0
            fox_k, fox_v = with_past(past[0][l], fk), with_past(past[1][l], fv)
            diff_k, diff_v = with_past(past[3][l], dk), with_past(past[4][l], dv)
            logf_all = jnp.pad(jnp.concatenate([past[2][l], fox_logf], axis=1),
                               ((0, 0), (0, pad), (0, 0)))
        c_rows = _cumsum_time(logf_all.transpose(1, 0, 2).reshape(tk_total, bh))
        c_cols = c_rows.T.reshape(bh, tk_total // tk, 1, tk)
        fox_o = _fox(zb, fox_k, fox_v, c_rows, c_cols, batch=batch, heads=heads,
                     tq_total=seq, tk_total=tk_total, past=plen, tq=tq, tk=tk)

        diff_o = _diff(zb, diff_k, diff_v, slopes, w["lq1"], w["lk1"], w["lq2"], w["lk2"],
                       w["diff_norm"], batch=batch, heads=heads, tq_total=seq, tk_total=tk_total,
                       past=plen, tq=tq, tk=tk, lam_init=lam_init)

        xf = _merge(hg_o, fox_o, diff_o, zb, xf, w["w_br_hgrn"], w["w_br_fox"], w["w_br_diff"],
                    w["w_out"], tm=tm)
        xf = _ffn(xf, w["norm_ffn"], w["w_gate_up"], w["w_down"], w["norm_final"], tm=tm,
                  norm_out=l == depth - 1)

        def heads4(a):
            return a.reshape(batch, seq, heads, HEAD)
        states.append((heads4(fk), heads4(fv), fox_logf, heads4(dk), heads4(dv), hg_s))

    stacked = tuple(jnp.stack(s, axis=0) for s in zip(*states))
    return xf.reshape(batch, seq, d), stacked


def kernel(x_prompt, x_sample, cache_fox_k, cache_fox_v, cache_fox_logf, cache_diff_k, cache_diff_v,
           state_hgrn, norm_mix, w_in, hg_lb_logits, hg_norm, fox_f_bias, diff_lam_q1, diff_lam_k1,
           diff_lam_q2, diff_lam_k2, diff_norm, w_br_hgrn, w_br_fox, w_br_diff, w_out, norm_ffn,
           w_gate_up, w_down, norm_final):
    depth, d, _ = w_in.shape
    wts = []
    for l in range(depth):
        w, wff = _prep_layer_weights(w_in[l], d)
        wts.append(dict(
            norm_mix=norm_mix[l][None, :], w_in=w, w_ff=wff,
            f_bias=jnp.pad(fox_f_bias[l], (0, HEAD - HEADS))[None, :],
            hg_norm=hg_norm[l][None, :],
            lq1=diff_lam_q1[l][None, :], lk1=diff_lam_k1[l][None, :],
            lq2=diff_lam_q2[l][None, :], lk2=diff_lam_k2[l][None, :],
            diff_norm=diff_norm[l][None, :],
            w_br_hgrn=w_br_hgrn[l].astype(BF16), w_br_fox=w_br_fox[l].astype(BF16),
            w_br_diff=w_br_diff[l].astype(BF16), w_out=w_out[l].astype(BF16),
            norm_ffn=norm_ffn[l][None, :],
            w_gate_up=w_gate_up[l].astype(BF16), w_down=w_down[l].astype(BF16),
            norm_final=norm_final[None, :],
        ))
    cs = jnp.cumsum(jax.nn.softmax(hg_lb_logits.astype(F32), axis=0), axis=0)
    lb_all = cs - cs[0:1]
    slope = jnp.exp2(-8.0 * jnp.arange(1, HEADS + 1, dtype=F32) / HEADS)
    slopes = jnp.broadcast_to(slope[:, None, None], (HEADS, 1, HEAD))

    y_p, st_p = _trunk(x_prompt, None, wts, lb_all, slopes)
    past = (cache_fox_k, cache_fox_v, cache_fox_logf, cache_diff_k, cache_diff_v, state_hgrn)
    y_s, st_s = _trunk(x_sample, past, wts, lb_all, slopes)
    return (y_p, y_s) + st_p + st_s
```

```python
import functools
import math

import jax
import jax.numpy as jnp
from jax import lax
from jax.experimental import pallas as pl
from jax.experimental.pallas import tpu as pltpu

EPS = 1e-6
CHUNK = 64
CHUNK_SHIFT = 6
assert 1 << CHUNK_SHIFT == CHUNK
HEAD = 128
HEADS = 8
HEADS_PER_STEP = 2
LOG2E = math.log2(math.e)
UNDERFLOW_LOG2 = 160.0

ZF_HQ, ZF_HF = range(2)
ZB_HI, ZB_HOG, ZB_FQ, ZB_DQ, ZB_GHG, ZB_GFOX, ZB_GDIFF = range(7)
ZS_FK, ZS_FV, ZS_DK, ZS_DV = range(4)
N_ZF, N_ZB, N_ZS = 2, 7, 4

V7X_VMEM_LIMIT_BYTES = 56 * 1024 * 1024

_NT = (((1,), (1,)), ((), ()))
_TN = (((0,), (0,)), ((), ()))
BF16 = jnp.bfloat16
F32 = jnp.float32


def _params(*semantics):
    return pltpu.CompilerParams(dimension_semantics=semantics,
                                vmem_limit_bytes=V7X_VMEM_LIMIT_BYTES)


def _sigmoid(x):
    return 1.0 / (1.0 + jnp.exp(-x))


def _log1p_exp_neg(a):
    return jnp.log(1.0 + jnp.exp(-a))


def _log_sigmoid(x):
    return jnp.minimum(x, 0.0) - _log1p_exp_neg(jnp.abs(x))


def _rms(x, g):
    var = jnp.mean(x * x, axis=-1, keepdims=True)
    return (x * lax.rsqrt(var + EPS)) * g


def _resident(shape):
    nd = len(shape)
    return pl.BlockSpec(shape, lambda *_: (0,) * nd)


def _split3(x):
    hi = x.astype(BF16).astype(F32)
    r = x - hi
    mid = r.astype(BF16).astype(F32)
    lo = (r - mid).astype(BF16).astype(F32)
    return hi, mid, lo


def _bias_lanes(pos, row_terms, one_lanes_first):
    t0, t1, t2 = row_terms
    base = 3 if one_lanes_first else 0
    ones_lo = 0 if one_lanes_first else 3
    out = jnp.where(pos == base, t0, jnp.where(pos == base + 1, t1, jnp.where(pos == base + 2, t2, 0.0)))
    return jnp.where(jnp.logical_and(pos >= ones_lo, pos < ones_lo + 3), 1.0, out)


def _proj_in_kernel(x_ref, g_ref, w_ref, wff_ref, fb_ref, zf_ref, zb_ref, fl_ref, h_ref):
    j = pl.program_id(1)

    @pl.when(j == 0)
    def _():
        h = _rms(x_ref[...], g_ref[...]).astype(BF16)
        h_ref[...] = h
        ff = jnp.dot(h, wff_ref[...], preferred_element_type=F32)
        fl_ref[...] = _log_sigmoid(ff + fb_ref[...])

    res = jnp.dot(h_ref[...], w_ref[...], preferred_element_type=F32)

    @pl.when(j < N_ZF)
    def _():
        zf_ref[...] = res

    @pl.when(j >= N_ZF)
    def _():
        zb_ref[...] = res.astype(BF16)


def _proj_in(x, g, w, wff, fb, *, tm):
    n, d = x.shape
    ngroups = N_ZF + N_ZB
    return pl.pallas_call(
        _proj_in_kernel,
        grid=(n // tm, ngroups),
        in_specs=[
            pl.BlockSpec((tm, d), lambda i, j: (i, 0)),
            _resident((1, d)),
            pl.BlockSpec((d, d), lambda i, j: (0, j)),
            _resident((d, HEAD)),
            _resident((1, HEAD)),
        ],
        out_specs=[
            pl.BlockSpec((None, tm, d), lambda i, j: (jnp.minimum(j, N_ZF - 1), i, 0)),
            pl.BlockSpec((None, tm, d), lambda i, j: (jnp.maximum(j - N_ZF, 0), i, 0)),
            pl.BlockSpec((tm, HEAD), lambda i, j: (i, 0)),
        ],
        out_shape=[
            jax.ShapeDtypeStruct((N_ZF, n, d), F32),
            jax.ShapeDtypeStruct((N_ZB, n, d), BF16),
            jax.ShapeDtypeStruct((n, HEAD), F32),
        ],
        scratch_shapes=[pltpu.VMEM((tm, d), BF16)],
        compiler_params=_params("parallel", "arbitrary"),
        name="proj_in",
    )(x, g, w, wff, fb)


def _proj_cache_kernel(x_ref, g_ref, w_ref, fk_ref, fv_ref, dk_ref, dv_ref, zkv_ref):
    h = _rms(x_ref[...], g_ref[...]).astype(BF16)
    d = x_ref.shape[1]
    for s, ref in enumerate((fk_ref, fv_ref, dk_ref, dv_ref)):
        res = jnp.dot(h, w_ref[:, s * d:(s + 1) * d], preferred_element_type=F32)
        ref[...] = res.reshape(ref.shape)
        zkv_ref[s] = res.astype(BF16)


def _proj_cache(x, g, w, *, tm):
    n, d = x.shape
    heads = d // HEAD
    cache_spec = pl.BlockSpec((tm, heads, HEAD), lambda i: (i, 0, 0))
    cache_shape = jax.ShapeDtypeStruct((n, heads, HEAD), F32)
    return pl.pallas_call(
        _proj_cache_kernel,
        grid=(n // tm,),
        in_specs=[pl.BlockSpec((tm, d), lambda i: (i, 0)), _resident((1, d)), _resident(w.shape)],
        out_specs=[cache_spec, cache_spec, cache_spec, cache_spec,
                   pl.BlockSpec((N_ZS, tm, d), lambda i: (0, i, 0))],
        out_shape=[cache_shape, cache_shape, cache_shape, cache_shape,
                   jax.ShapeDtypeStruct((N_ZS, n, d), BF16)],
        compiler_params=_params("parallel"),
        name="proj_cache",
    )(x, g, w)


def _hgrn_kernel(zq_ref, zf_ref, zi_ref, zog_ref, lb_ref, gn_ref, s0_ref,
                 o_ref, sout_ref, st_ref, *, chunk, heads):
    t = pl.program_id(1)
    tt = zq_ref.shape[0]

    @pl.when(t == 0)
    def _():
        for h in range(heads):
            st_ref[h] = s0_ref[0, h].T

    lb = lb_ref[...]
    log_lb = jnp.log(lb)
    log_1m_lb = jnp.log1p(-lb)
    gn = gn_ref[...]
    row = lax.broadcasted_iota(jnp.int32, (chunk, chunk), 0)
    col = lax.broadcasted_iota(jnp.int32, (chunk, chunk), 1)
    causal = row >= col
    tril = causal.astype(F32)
    mid = chunk // 2

    def one_chunk(c, carry):
        r0 = pl.multiple_of(c * chunk, chunk)
        rows = pl.ds(r0, chunk)
        zf = zf_ref[rows, :]
        b = log_1m_lb + _log_sigmoid(zf)
        logf = jnp.maximum(log_lb, b) + _log1p_exp_neg(jnp.abs(log_lb - b))
        kk = (1.0 - lb) * _sigmoid(-zf)
        L = jnp.dot(tril, logf, preferred_element_type=F32, precision=lax.Precision.HIGHEST)
        for h in range(heads):
            sl = slice(h * HEAD, (h + 1) * HEAD)
            Lh = L[:, sl]
            Lr = Lh[mid:mid + 1, :]
            Lend = Lh[chunk - 1:chunk, :]
            q = zq_ref[rows, sl]
            k = kk[:, sl]
            v = zi_ref[rows, sl]
            q_in = (q * jnp.exp(Lh)).astype(BF16)
            q_r = (q * jnp.exp(Lh - Lr)).astype(BF16)
            k_r = (k * jnp.exp(Lr - Lh)).astype(BF16)
            k_end = (k * jnp.exp(Lend - Lh)).astype(BF16)
            a = lax.dot_general(q_r, k_r, _NT, preferred_element_type=F32)
            a = jnp.where(causal, a, 0.0).astype(BF16)
            st = st_ref[h]
            o = (lax.dot_general(q_in, st.astype(BF16), _NT, preferred_element_type=F32)
                 + jnp.dot(a, v, preferred_element_type=F32))
            st_ref[h] = st * jnp.exp(Lend) + lax.dot_general(v, k_end, _TN, preferred_element_type=F32)
            og = zog_ref[rows, sl].astype(F32)
            o_ref[rows, sl] = (_rms(o, gn) * (og * _sigmoid(og))).astype(o_ref.dtype)
        return carry

    nchunks = tt // chunk
    lax.fori_loop(0, nchunks, one_chunk, 0, unroll=2 if nchunks % 2 == 0 else 1)

    @pl.when(t == pl.num_programs(1) - 1)
    def _():
        for h in range(heads):
            sout_ref[0, h] = st_ref[h].T


def _hgrn(zf, zb, lb, gn, s0, *, batch, seq, chunk, tt):
    _, n, d = zf.shape
    heads = d // HEAD
    nt = seq // tt

    def zspec(slot):
        return pl.BlockSpec((None, tt, d), lambda b, t: (slot, b * nt + t, 0))

    state_spec = pl.BlockSpec((1, heads, HEAD, HEAD), lambda b, t: (b, 0, 0, 0))
    return pl.pallas_call(
        functools.partial(_hgrn_kernel, chunk=chunk, heads=heads),
        grid=(batch, nt),
        in_specs=[zspec(ZF_HQ), zspec(ZF_HF), zspec(ZB_HI), zspec(ZB_HOG),
                  _resident((1, d)), _resident((1, HEAD)), state_spec],
        out_specs=[pl.BlockSpec((tt, d), lambda b, t: (b * nt + t, 0)), state_spec],
        out_shape=[jax.ShapeDtypeStruct((n, d), BF16),
                   jax.ShapeDtypeStruct((batch, heads, HEAD, HEAD), F32)],
        scratch_shapes=[pltpu.VMEM((heads, HEAD, HEAD), F32)],
        compiler_params=_params("parallel", "arbitrary"),
        name="hgrn",
    )(zf, zf, zb, zb, lb, gn, s0)


def _cumsum_kernel(x_ref, o_ref, *, blk):
    nblk = x_ref.shape[0] // blk
    row = lax.broadcasted_iota(jnp.int32, (blk, blk), 0)
    col = lax.broadcasted_iota(jnp.int32, (blk, blk), 1)
    tril = (row >= col).astype(F32)

    def body(i, carry):
        rows = pl.ds(pl.multiple_of(i * blk, blk), blk)
        c = jnp.dot(tril, x_ref[rows, :], preferred_element_type=F32,
                    precision=lax.Precision.HIGHEST) + carry
        o_ref[rows, :] = c
        return c[blk - 1:blk, :]

    lax.fori_loop(0, nblk, body, jnp.zeros((1, x_ref.shape[1]), F32))


def _cumsum_time(x):
    return pl.pallas_call(
        functools.partial(_cumsum_kernel, blk=128),
        out_shape=jax.ShapeDtypeStruct(x.shape, F32),
        compiler_params=_params(),
        name="fox_cumsum",
    )(x)


STRIP = 64


def _lane_tiles(x, n):
    return x if n == 1 else jnp.concatenate([x] * n, axis=1)


def _online_softmax_block(c, q_aug, kt_blk, vblk, refs, fix_scores=None, aligned_diagonal=False):
    s_ref, p_ref, m_ref, acc_ref = refs
    tq, tk = s_ref.shape[1], s_ref.shape[2]
    strip = min(tq, STRIP)
    s_ref[c] = jnp.dot(q_aug, kt_blk, preferred_element_type=F32)
    for r0 in range(0, tq, strip):
        rs = slice(r0, r0 + strip)
        if aligned_diagonal:
            lo = r0 // HEAD * HEAD
            hi = (r0 + strip - 1) // HEAD * HEAD + HEAD
        else:
            lo, hi = 0, tk
        s = s_ref[c, rs, :hi]
        if fix_scores is not None:
            fixed = fix_scores(s[:, lo:], r0, strip, lo, hi - lo)
            s = fixed if lo == 0 else jnp.concatenate([s[:, :lo], fixed], axis=1)
        m_prev = m_ref[c, rs, :]
        m_new = jnp.maximum(m_prev, jnp.max(s, axis=1, keepdims=True))
        alpha = jnp.exp2(m_prev - m_new)
        p = jnp.exp2(s - _lane_tiles(m_new, hi // HEAD))
        m_ref[c, rs, :] = m_new
        acc_ref[c, rs, :] = _lane_tiles(alpha, 2) * acc_ref[c, rs, :]
        p_ref[c, rs, :hi] = p.astype(BF16)
        if hi < tk:
            p_ref[c, rs, hi:] = jnp.zeros((strip, tk - hi), BF16)
    acc_ref[c] += jnp.dot(p_ref[c], vblk, preferred_element_type=F32)


def _softmax_init(refs):
    _, _, m_ref, acc_ref = refs
    m_ref[...] = jnp.full(m_ref.shape, -jnp.inf, F32)
    acc_ref[...] = jnp.zeros(acc_ref.shape, F32)


def _softmax_result(refs, c):
    acc_ref = refs[3]
    return acc_ref[c, :, :HEAD] / acc_ref[c, :, HEAD:]


def _values_with_ones(v_ref, vb_ref):
    rows = v_ref.shape[0]
    for hh in range(v_ref.shape[1] // HEAD):
        vb_ref[:, 2 * hh * HEAD:(2 * hh + 1) * HEAD] = v_ref[:, hh * HEAD:(hh + 1) * HEAD].astype(BF16)
        vb_ref[:, (2 * hh + 1) * HEAD:(2 * hh + 2) * HEAD] = jnp.ones((rows, HEAD), BF16)


def _softmax_scratch(chains, tq, tk):
    return [pltpu.VMEM((chains, tq, tk), F32), pltpu.VMEM((chains, tq, tk), BF16),
            pltpu.VMEM((chains, tq, HEAD), F32), pltpu.VMEM((chains, tq, 2 * HEAD), F32)]


def _loop_two_per_trip(lo, hi, body):
    pairs = (hi - lo) // 2

    def two(t, carry):
        j = lo + 2 * t
        body(j, 0)
        body(j + 1, 0)
        return carry

    lax.fori_loop(0, pairs, two, 0)
    tail = lo + 2 * pairs

    @pl.when(tail < hi)
    def _():
        body(tail, 0)


def _block_counts(past, i, tq, tk, k_valid):
    n_before = (past + i * tq) // tk
    kmax = jnp.minimum(((past + (i + 1) * tq - 1) // CHUNK + 1) * CHUNK, k_valid)
    return n_before, (kmax + tk - 1) // tk


def _fox_kernel(q_ref, k_ref, v_ref, cq_ref, ck_ref, o_ref, kt_ref, vb_ref, stats_ref, *refs,
                heads, past, k_valid, tk):
    b, hp, i = pl.program_id(0), pl.program_id(1), pl.program_id(2)
    tq = q_ref.shape[0]
    hps = q_ref.shape[1] // HEAD
    nkb = kt_ref.shape[1]
    sub = tk // HEAD

    blk = lax.broadcasted_iota(jnp.int32, (1, HEAD), 1)
    assert nkb <= HEAD

    @pl.when(i == 0)
    def _():
        _values_with_ones(v_ref, vb_ref)
        stats_ref[...] = jnp.zeros(stats_ref.shape, F32)
        row = lax.broadcasted_iota(jnp.int32, (HEAD, tk), 0)
        for hh in range(hps):
            def fill(jb, carry, hh=hh):
                r0 = pl.multiple_of(jb * tk, tk)
                per_key = jnp.zeros((1, HEAD), F32)
                for u in range(sub):
                    kblk = k_ref[pl.ds(r0 + u * HEAD, HEAD), hh * HEAD:(hh + 1) * HEAD]
                    kt = kblk.astype(F32).T.astype(BF16)
                    kt_ref[hh, jb, 0:HEAD, u * HEAD:(u + 1) * HEAD] = kt
                    ktf = kt.astype(F32)
                    per_key = jnp.maximum(per_key, jnp.sum(ktf * ktf, axis=0, keepdims=True))
                norm2 = jnp.max(per_key, axis=1, keepdims=True)
                ck = LOG2E * ck_ref[hh, jb]
                kt_ref[hh, jb, HEAD:2 * HEAD, :] = _bias_lanes(row, _split3(-ck), True).astype(BF16)
                here = blk == jb
                stats_ref[hh, 0:1, :] = jnp.where(here, norm2, stats_ref[hh, 0:1, :])
                stats_ref[hh, 1:2, :] = jnp.where(here, jnp.min(ck, axis=1, keepdims=True),
                                                  stats_ref[hh, 1:2, :])
                return carry
            lax.fori_loop(0, nkb, fill, 0)

    def first_needed_block(hh, cq, n_before):
        q32 = q_ref[:, hh * HEAD:(hh + 1) * HEAD].astype(F32)
        nq2 = jnp.max(jnp.sum(q32 * q32, axis=1, keepdims=True), axis=0, keepdims=True)
        reach = jnp.sqrt(nq2 * stats_ref[hh, 0:1, :])
        own = jnp.max(jnp.where(blk == n_before, reach, 0.0), axis=1, keepdims=True)
        upper = reach + jnp.max(cq, axis=0, keepdims=True) - stats_ref[hh, 1:2, :] + own
        needed = jnp.logical_or(jnp.logical_not(upper < -UNDERFLOW_LOG2), blk >= n_before)
        return jnp.min(jnp.where(needed, blk, HEAD))

    lane = lax.broadcasted_iota(jnp.int32, (tq, HEAD), 1)
    qpos = past + i * tq + lax.broadcasted_iota(jnp.int32, (tq, 1), 0)
    kiota = lax.broadcasted_iota(jnp.int32, (1, tk), 1)
    cq_tile = cq_ref[...]
    pair = lax.broadcasted_iota(jnp.int32, cq_tile.shape, 1)
    aligned = tq == tk and past % tk == 0
    n_before, nk = _block_counts(past, i, tq, tk, k_valid)
    q_aug = []
    j_start = n_before if aligned else 0
    for hh in range(hps):
        bh = b * heads + hp * hps + hh
        cq = LOG2E * jnp.sum(jnp.where(pair == bh, cq_tile, 0.0), axis=1, keepdims=True)
        aug = _bias_lanes(lane, _split3(cq), False).astype(BF16)
        q_aug.append(jnp.concatenate([q_ref[:, hh * HEAD:(hh + 1) * HEAD], aug], axis=1))
        if aligned:
            j_start = jnp.minimum(j_start, first_needed_block(hh, cq, n_before))

    def make_body(masked):
        def body(j, carry):
            rows = pl.ds(pl.multiple_of(j * tk, tk), tk)

            def causal(s, r0, strip, c0, cols):
                kpos = j * tk + kiota[:, c0:c0 + cols]
                return jnp.where(kpos <= qpos[r0:r0 + strip], s, -jnp.inf)

            for hh in range(hps):
                _online_softmax_block(hh, q_aug[hh], kt_ref[hh, j],
                                      vb_ref[rows, 2 * hh * HEAD:(2 * hh + 2) * HEAD], refs,
                                      causal if masked else None, masked and aligned)
            return carry
        return body

    _softmax_init(refs)
    _loop_two_per_trip(j_start, n_before, make_body(False))
    lax.fori_loop(n_before, nk, make_body(True), 0)
    for hh in range(hps):
        o_ref[:, hh * HEAD:(hh + 1) * HEAD] = _softmax_result(refs, hh).astype(o_ref.dtype)


def _fox(zb, k, v, c_rows, c_cols, *, batch, heads, tq_total, tk_total, past, tq, tk):
    _, n, d = zb.shape
    hps = HEADS_PER_STEP
    nq = tq_total // tq
    nkb = tk_total // tk
    (k, k_slot), (v, v_slot) = k, v

    def kv_spec(slot):
        return pl.BlockSpec((None, tk_total, hps * HEAD), lambda b, h, i: (slot, b, h))
    return pl.pallas_call(
        functools.partial(_fox_kernel, heads=heads, past=past, k_valid=past + tq_total, tk=tk),
        grid=(batch, heads // hps, nq),
        in_specs=[
            pl.BlockSpec((None, tq, hps * HEAD), lambda b, h, i: (ZB_FQ, b * nq + i, h)),
            kv_spec(k_slot), kv_spec(v_slot),
            pl.BlockSpec((tq, batch * heads), lambda b, h, i: (past // tq + i, 0)),
            pl.BlockSpec((hps, nkb, 1, tk), lambda b, h, i: (b * (heads // hps) + h, 0, 0, 0)),
        ],
        out_specs=pl.BlockSpec((tq, hps * HEAD), lambda b, h, i: (b * nq + i, h)),
        out_shape=jax.ShapeDtypeStruct((n, d), BF16),
        scratch_shapes=[pltpu.VMEM((hps, nkb, 2 * HEAD, tk), BF16),
                        pltpu.VMEM((tk_total, 2 * hps * HEAD), BF16),
                        pltpu.VMEM((hps, 8, HEAD), F32)] + _softmax_scratch(hps, tq, tk),
        compiler_params=_params("parallel", "parallel", "arbitrary"),
        name="fox_attn",
    )(zb, k, v, c_rows, c_cols)


def _diff_kernel(q_ref, k_ref, v_ref, slope_ref, lq1_ref, lk1_ref, lq2_ref, lk2_ref, gn_ref,
                 o_ref, kt_ref, vb_ref, stats_ref, *refs, past, k_valid, tk, lam_init):
    i = pl.program_id(2)
    tq = q_ref.shape[0]
    hps = q_ref.shape[1] // HEAD
    nkb = kt_ref.shape[2]
    sub = tk // HEAD
    dq = HEAD // 2
    blk = lax.broadcasted_iota(jnp.int32, (1, HEAD), 1)
    assert nkb <= HEAD

    @pl.when(i == 0)
    def _():
        _values_with_ones(v_ref, vb_ref)
        stats_ref[...] = jnp.zeros(stats_ref.shape, F32)
        row = lax.broadcasted_iota(jnp.int32, (HEAD, HEAD), 0)
        col = lax.broadcasted_iota(jnp.int32, (1, HEAD), 1)
        for hh in range(hps):
            slope2 = LOG2E * slope_ref[hh]

            def fill(jb, carry, hh=hh, slope2=slope2):
                r0 = pl.multiple_of(jb * tk, tk)
                per_key = jnp.zeros((1, HEAD), F32)
                for u in range(sub):
                    kblk = k_ref[pl.ds(r0 + u * HEAD, HEAD), hh * HEAD:(hh + 1) * HEAD]
                    kt = kblk.astype(BF16).astype(F32).T
                    per_key = jnp.maximum(per_key, jnp.sum(kt * kt, axis=0, keepdims=True))
                    kpos = (r0 + u * HEAD + col).astype(F32)
                    col_bias = _split3(slope2 * kpos)
                    lanes = slice(u * HEAD, (u + 1) * HEAD)
                    kt_ref[hh, 0, jb, :, lanes] = jnp.where(
                        row < dq, kt, _bias_lanes(row - dq, col_bias, True)).astype(BF16)
                    kt_ref[hh, 1, jb, :, lanes] = jnp.where(
                        row >= dq, kt, _bias_lanes(row, col_bias, True)).astype(BF16)
                stats_ref[hh, 0:1, :] = jnp.where(blk == jb, jnp.max(per_key, axis=1, keepdims=True),
                                                  stats_ref[hh, 0:1, :])
                return carry
            lax.fori_loop(0, nkb, fill, 0)

    def first_needed_block(hh, slope2, n_before):
        q32 = q_ref[:, hh * HEAD:(hh + 1) * HEAD].astype(F32)
        nq2 = jnp.max(jnp.sum(q32 * q32, axis=1, keepdims=True), axis=0, keepdims=True)
        reach = jnp.sqrt(nq2 * stats_ref[hh, 0:1, :])
        own = jnp.max(jnp.where(blk == n_before, reach, 0.0), axis=1, keepdims=True)
        nearest = (past + i * tq - (blk + 1) * tk + 1).astype(F32)
        upper = reach - slope2 * nearest + own
        needed = jnp.logical_or(jnp.logical_not(upper < -UNDERFLOW_LOG2), blk >= n_before)
        return jnp.min(jnp.where(needed, blk, HEAD))

    lam = (jnp.exp(jnp.sum(lq1_ref[...] * lk1_ref[...], axis=1, keepdims=True))
           - jnp.exp(jnp.sum(lq2_ref[...] * lk2_ref[...], axis=1, keepdims=True)) + lam_init)

    lane = lax.broadcasted_iota(jnp.int32, (tq, HEAD), 1)
    qpos = past + i * tq + lax.broadcasted_iota(jnp.int32, (tq, 1), 0)
    qposf = qpos.astype(F32)
    qchunk = lax.shift_right_logical(qpos, CHUNK_SHIFT)
    kiota = lax.broadcasted_iota(jnp.int32, (1, tk), 1)
    aligned = tq == tk and past % tk == 0
    n_before, nk = _block_counts(past, i, tq, tk, k_valid)
    j_start = n_before if aligned else 0
    q_aug, slope_col = [], []
    for hh in range(hps):
        slope2 = LOG2E * slope_ref[hh][:, 0:1]
        slope_col.append(slope2)
        if aligned:
            j_start = jnp.minimum(j_start, first_needed_block(hh, slope2, n_before))
        row_bias = _split3(-slope2 * qposf)
        q = q_ref[:, hh * HEAD:(hh + 1) * HEAD].astype(F32)
        q1 = jnp.where(lane < dq, q, _bias_lanes(lane - dq, row_bias, False)).astype(BF16)
        q2 = jnp.where(lane >= dq, q, _bias_lanes(lane, row_bias, False)).astype(BF16)
        q_aug.append((q1, q2))

    def make_body(diagonal):
        def body(j, carry):
            rows = pl.ds(pl.multiple_of(j * tk, tk), tk)
            for hh in range(hps):
                vblk = vb_ref[rows, 2 * hh * HEAD:(2 * hh + 2) * HEAD]

                def fix(s, r0, strip, c0, cols, hh=hh):
                    kpos = j * tk + kiota[:, c0:c0 + cols]
                    ahead = jnp.maximum(kpos - qpos[r0:r0 + strip], 0).astype(F32)
                    visible = jnp.logical_and(
                        lax.shift_right_logical(kpos, CHUNK_SHIFT) <= qchunk[r0:r0 + strip],
                        kpos < k_valid)
                    return jnp.where(visible, s - (2.0 * slope_col[hh]) * ahead, -jnp.inf)

                for mp in range(2):
                    _online_softmax_block(2 * hh + mp, q_aug[hh][mp], kt_ref[hh, mp, j], vblk, refs,
                                          fix if diagonal else None, diagonal and aligned)
            return carry
        return body

    _softmax_init(refs)
    _loop_two_per_trip(j_start, n_before, make_body(False))
    lax.fori_loop(n_before, nk, make_body(True), 0)
    for hh in range(hps):
        o = _softmax_result(refs, 2 * hh) - lam * _softmax_result(refs, 2 * hh + 1)
        o_ref[:, hh * HEAD:(hh + 1) * HEAD] = (
            _rms(o, gn_ref[...]) * (1.0 - lam_init)).astype(o_ref.dtype)


def _diff(zb, k, v, slopes, lq1, lk1, lq2, lk2, gn, *, batch, heads, tq_total, tk_total, past,
          tq, tk, lam_init):
    _, n, d = zb.shape
    hps = HEADS_PER_STEP
    nq = tq_total // tq
    nkb = tk_total // tk
    (k, k_slot), (v, v_slot) = k, v

    def kv_spec(slot):
        return pl.BlockSpec((None, tk_total, hps * HEAD), lambda b, h, i: (slot, b, h))
    lam_spec = _resident((1, HEAD // 2))
    return pl.pallas_call(
        functools.partial(_diff_kernel, past=past, k_valid=past + tq_total, tk=tk, lam_init=lam_init),
        grid=(batch, heads // hps, nq),
        in_specs=[
            pl.BlockSpec((None, tq, hps * HEAD), lambda b, h, i: (ZB_DQ, b * nq + i, h)),
            kv_spec(k_slot), kv_spec(v_slot),
            pl.BlockSpec((hps, 1, HEAD), lambda b, h, i: (h, 0, 0)),
            lam_spec, lam_spec, lam_spec, lam_spec,
            _resident((1, HEAD)),
        ],
        out_specs=pl.BlockSpec((tq, hps * HEAD), lambda b, h, i: (b * nq + i, h)),
        out_shape=jax.ShapeDtypeStruct((n, d), BF16),
        scratch_shapes=[pltpu.VMEM((hps, 2, nkb, HEAD, tk), BF16),
                        pltpu.VMEM((tk_total, 2 * hps * HEAD), BF16),
                        pltpu.VMEM((hps, 8, HEAD), F32)] + _softmax_scratch(2 * hps, tq, tk),
        compiler_params=_params("parallel", "parallel", "arbitrary"),
        name="diff_attn",
    )(zb, k, v, slopes, lq1, lk1, lq2, lk2, gn)


def _merge_kernel(a1_ref, a2_ref, a3_ref, g1_ref, g2_ref, g3_ref, x_ref,
                  w1_ref, w2_ref, w3_ref, wo_ref, o_ref):
    def branch(a_ref, g_ref, w_ref):
        return _sigmoid(g_ref[...].astype(F32)) * jnp.dot(a_ref[...], w_ref[...],
                                                          preferred_element_type=F32)

    mix = branch(a1_ref, g1_ref, w1_ref) + branch(a2_ref, g2_ref, w2_ref) + branch(a3_ref, g3_ref, w3_ref)
    y = jnp.dot(mix.astype(BF16), wo_ref[...], preferred_element_type=F32)
    o_ref[...] = x_ref[...] + y


def _merge(a1, a2, a3, zb, x, w1, w2, w3, wo, *, tm):
    n, d = x.shape
    act = pl.BlockSpec((tm, d), lambda i: (i, 0))

    def gate(slot):
        return pl.BlockSpec((None, tm, d), lambda i: (slot, i, 0))

    wspec = _resident((d, d))
    return pl.pallas_call(
        _merge_kernel,
        grid=(n // tm,),
        in_specs=[act, act, act, gate(ZB_GHG), gate(ZB_GFOX), gate(ZB_GDIFF), act,
                  wspec, wspec, wspec, wspec],
        out_specs=act,
        out_shape=jax.ShapeDtypeStruct((n, d), F32),
        compiler_params=_params("parallel"),
        name="merge_out",
    )(a1, a2, a3, zb, zb, zb, x, w1, w2, w3, wo)


def _ffn_kernel(x_ref, g_ref, wgu_ref, wd_ref, gout_ref, o_ref, *, d_ff, fchunk, norm_out):
    x = x_ref[...]
    h = _rms(x, g_ref[...]).astype(BF16)
    acc = x
    for c0 in range(0, d_ff, fchunk):
        a = jnp.dot(h, wgu_ref[:, c0:c0 + fchunk], preferred_element_type=F32)
        b = jnp.dot(h, wgu_ref[:, d_ff + c0:d_ff + c0 + fchunk], preferred_element_type=F32)
        act = (a * _sigmoid(a) * b).astype(BF16)
        acc = acc + jnp.dot(act, wd_ref[c0:c0 + fchunk, :], preferred_element_type=F32)
    o_ref[...] = _rms(acc, gout_ref[...]) if norm_out else acc


def _ffn(x, g, wgu, wd, g_out, *, tm, norm_out):
    n, d = x.shape
    d_ff = wd.shape[0]
    fchunk = d_ff // 2
    assert fchunk % HEAD == 0
    act = pl.BlockSpec((tm, d), lambda i: (i, 0))
    return pl.pallas_call(
        functools.partial(_ffn_kernel, d_ff=d_ff, fchunk=fchunk, norm_out=norm_out),
        grid=(n // tm,),
        in_specs=[act, _resident((1, d)), _resident(wgu.shape), _resident(wd.shape), _resident((1, d))],
        out_specs=act,
        out_shape=jax.ShapeDtypeStruct((n, d), F32),
        compiler_params=_params("parallel"),
        name="ffn",
    )(x, g, wgu, wd, g_out)


def _prep_layer_weights(w_in_l, d):
    hd = HEADS
    splits = (d, d, d, d, d, d, d, hd, d, d, d, d, d, d)
    offs = [0]
    for s in splits:
        offs.append(offs[-1] + s)
    cols = [w_in_l[:, offs[k]:offs[k + 1]] for k in range(len(splits))]
    hq, hf, hi, hog, fq, fk, fv, ff, dq, dk, dv, g_hg, g_fox, g_diff = cols
    fq = fq * (HEAD ** -0.5 * LOG2E)
    dq = dq * ((HEAD // 2) ** -0.5 * LOG2E)
    w = jnp.concatenate([hq, hf, hi, hog, fq, dq, g_hg, g_fox, g_diff], axis=1).astype(BF16)
    w_cache = jnp.concatenate([fk, fv, dk, dv], axis=1).astype(BF16)
    wff = jnp.pad(ff, ((0, 0), (0, HEAD - hd))).astype(BF16)
    return w, w_cache, wff


def _pick_tile(n, want):
    t = min(n, want)
    assert n % t == 0
    return t


def _trunk(x, past, wts, lb_all, slopes):
    batch, seq, d = x.shape
    n = batch * seq
    heads = d // HEAD
    bh = batch * heads
    depth = len(wts)
    prompt = past is None
    plen = 0 if prompt else past[0].shape[2]
    tm = _pick_tile(n, 512)
    tm_proj = _pick_tile(n, 1024)

    if prompt:
        chunk, tt = CHUNK, _pick_tile(seq, 512)
        tq = tk = _pick_tile(seq, 512)
        tk_total = seq
    else:
        chunk, tt = seq, seq
        tq = seq
        tk = tk_total = -(-(plen + seq) // HEAD) * HEAD
    assert plen % tq == 0
    pad = tk_total - (plen + seq)

    xf = x.reshape(n, d)
    states = []
    for l in range(depth):
        w = wts[l]
        lam_init = 0.8 - 0.6 * math.exp(-0.3 * l)
        zf, zb, fl = _proj_in(xf, w["norm_mix"], w["w_in"], w["w_ff"], w["f_bias"], tm=tm_proj)
        fk, fv, dk, dv, zs = _proj_cache(xf, w["norm_mix"], w["w_cache"], tm=tm)
        fox_logf = fl[:, :heads].reshape(batch, seq, heads)

        if prompt:
            s0 = jnp.zeros((batch, heads, HEAD, HEAD), F32)
        else:
            s0 = past[5][l]
        hg_o, hg_s = _hgrn(zf, zb, lb_all[l][None, :], w["hg_norm"], s0,
                           batch=batch, seq=seq, chunk=chunk, tt=tt)

        if prompt:
            fox_k, fox_v = (zs, ZS_FK), (zs, ZS_FV)
            diff_k, diff_v = (zs, ZS_DK), (zs, ZS_DV)
            logf_all = fox_logf
        else:
            def with_past(cache, new):
                new4 = new.reshape(batch, seq, heads, HEAD)
                full = jnp.concatenate([cache, new4], axis=1)
                full = jnp.pad(full, ((0, 0), (0, pad), (0, 0), (0, 0)))
                return full.reshape(1, batch * tk_total, d), 0
            fox_k, fox_v = with_past(past[0][l], fk), with_past(past[1][l], fv)
            diff_k, diff_v = with_past(past[3][l], dk), with_past(past[4][l], dv)
            logf_all = jnp.pad(jnp.concatenate([past[2][l], fox_logf], axis=1),
                               ((0, 0), (0, pad), (0, 0)))
        c_rows = _cumsum_time(logf_all.transpose(1, 0, 2).reshape(tk_total, bh))
        c_cols = c_rows.T.reshape(bh, tk_total // tk, 1, tk)
        fox_o = _fox(zb, fox_k, fox_v, c_rows, c_cols, batch=batch, heads=heads,
                     tq_total=seq, tk_total=tk_total, past=plen, tq=tq, tk=tk)

        diff_o = _diff(zb, diff_k, diff_v, slopes, w["lq1"], w["lk1"], w["lq2"], w["lk2"],
                       w["diff_norm"], batch=batch, heads=heads, tq_total=seq, tk_total=tk_total,
                       past=plen, tq=tq, tk=tk, lam_init=lam_init)

        xf = _merge(hg_o, fox_o, diff_o, zb, xf, w["w_br_hgrn"], w["w_br_fox"], w["w_br_diff"],
                    w["w_out"], tm=tm)
        xf = _ffn(xf, w["norm_ffn"], w["w_gate_up"], w["w_down"], w["norm_final"], tm=tm,
                  norm_out=l == depth - 1)

        def heads4(a):
            return a.reshape(batch, seq, heads, HEAD)
        states.append((heads4(fk), heads4(fv), fox_logf, heads4(dk), heads4(dv), hg_s))

    stacked = tuple(jnp.stack(s, axis=0) for s in zip(*states))
    return xf.reshape(batch, seq, d), stacked


def kernel(x_prompt, x_sample, cache_fox_k, cache_fox_v, cache_fox_logf, cache_diff_k, cache_diff_v,
           state_hgrn, norm_mix, w_in, hg_lb_logits, hg_norm, fox_f_bias, diff_lam_q1, diff_lam_k1,
           diff_lam_q2, diff_lam_k2, diff_norm, w_br_hgrn, w_br_fox, w_br_diff, w_out, norm_ffn,
           w_gate_up, w_down, norm_final):
    depth, d, _ = w_in.shape
    wts = []
    for l in range(depth):
        w, w_cache, wff = _prep_layer_weights(w_in[l], d)
        wts.append(dict(
            norm_mix=norm_mix[l][None, :], w_in=w, w_cache=w_cache, w_ff=wff,
            f_bias=jnp.pad(fox_f_bias[l], (0, HEAD - HEADS))[None, :],
            hg_norm=hg_norm[l][None, :],
            lq1=diff_lam_q1[l][None, :], lk1=diff_lam_k1[l][None, :],
            lq2=diff_lam_q2[l][None, :], lk2=diff_lam_k2[l][None, :],
            diff_norm=diff_norm[l][None, :],
            w_br_hgrn=w_br_hgrn[l].astype(BF16), w_br_fox=w_br_fox[l].astype(BF16),
            w_br_diff=w_br_diff[l].astype(BF16), w_out=w_out[l].astype(BF16),
            norm_ffn=norm_ffn[l][None, :],
            w_gate_up=w_gate_up[l].astype(BF16), w_down=w_down[l].astype(BF16),
            norm_final=norm_final[None, :],
        ))
    cs = jnp.cumsum(jax.nn.softmax(hg_lb_logits.astype(F32), axis=0), axis=0)
    lb_all = cs - cs[0:1]
    slope = jnp.exp2(-8.0 * jnp.arange(1, HEADS + 1, dtype=F32) / HEADS)
    slopes = jnp.broadcast_to(slope[:, None, None], (HEADS, 1, HEAD))

    y_p, st_p = _trunk(x_prompt, None, wts, lb_all, slopes)
    past = (cache_fox_k, cache_fox_v, cache_fox_logf, cache_diff_k, cache_diff_v, state_hgrn)
    y_s, st_s = _trunk(x_sample, past, wts, lb_all, slopes)
    return (y_p, y_s) + st_p + st_s
```

```python
import functools
import math

import jax
import jax.numpy as jnp
from jax import lax
from jax.experimental import pallas as pl
from jax.experimental.pallas import tpu as pltpu

EPS = 1e-6
CHUNK = 64
CHUNK_SHIFT = 6
assert 1 << CHUNK_SHIFT == CHUNK
HEAD = 128
HEADS = 8
HEADS_PER_STEP = 4
LOG2E = math.log2(math.e)
UNDERFLOW_LOG2 = 160.0

ZF_HQ, ZF_HF = range(2)
ZB_HI, ZB_HOG, ZB_FQ, ZB_DQ, ZB_GHG, ZB_GFOX, ZB_GDIFF = range(7)
ZS_FK, ZS_FV, ZS_DK, ZS_DV = range(4)
N_ZF, N_ZB, N_ZS = 2, 7, 4

V7X_VMEM_LIMIT_BYTES = 60 * 1024 * 1024

_NT = (((1,), (1,)), ((), ()))
_TN = (((0,), (0,)), ((), ()))
BF16 = jnp.bfloat16
F32 = jnp.float32


def _params(*semantics):
    return pltpu.CompilerParams(dimension_semantics=semantics,
                                vmem_limit_bytes=V7X_VMEM_LIMIT_BYTES)


def _sigmoid(x):
    return 1.0 / (1.0 + jnp.exp(-x))


def _log1p_exp_neg(a):
    return jnp.log(1.0 + jnp.exp(-a))


def _log_sigmoid(x):
    return jnp.minimum(x, 0.0) - _log1p_exp_neg(jnp.abs(x))


def _rms(x, g):
    var = jnp.mean(x * x, axis=-1, keepdims=True)
    return (x * lax.rsqrt(var + EPS)) * g


def _resident(shape):
    nd = len(shape)
    return pl.BlockSpec(shape, lambda *_: (0,) * nd)


def _split3(x):
    hi = x.astype(BF16).astype(F32)
    r = x - hi
    mid = r.astype(BF16).astype(F32)
    lo = (r - mid).astype(BF16).astype(F32)
    return hi, mid, lo


def _bias_lanes(pos, row_terms, one_lanes_first):
    t0, t1, t2 = row_terms
    base = 3 if one_lanes_first else 0
    ones_lo = 0 if one_lanes_first else 3
    out = jnp.where(pos == base, t0, jnp.where(pos == base + 1, t1, jnp.where(pos == base + 2, t2, 0.0)))
    return jnp.where(jnp.logical_and(pos >= ones_lo, pos < ones_lo + 3), 1.0, out)


def _proj_in_kernel(x_ref, g_ref, w_ref, wff_ref, fb_ref, zf_ref, zb_ref, fl_ref, h_ref):
    j = pl.program_id(1)

    @pl.when(j == 0)
    def _():
        h = _rms(x_ref[...], g_ref[...]).astype(BF16)
        h_ref[...] = h
        ff = jnp.dot(h, wff_ref[...], preferred_element_type=F32)
        fl_ref[...] = _log_sigmoid(ff + fb_ref[...])

    res = jnp.dot(h_ref[...], w_ref[...], preferred_element_type=F32)

    @pl.when(j < N_ZF)
    def _():
        zf_ref[...] = res

    @pl.when(j >= N_ZF)
    def _():
        zb_ref[...] = res.astype(BF16)


def _proj_in(x, g, w, wff, fb, *, tm):
    n, d = x.shape
    ngroups = N_ZF + N_ZB
    return pl.pallas_call(
        _proj_in_kernel,
        grid=(n // tm, ngroups),
        in_specs=[
            pl.BlockSpec((tm, d), lambda i, j: (i, 0)),
            _resident((1, d)),
            pl.BlockSpec((d, d), lambda i, j: (0, j)),
            _resident((d, HEAD)),
            _resident((1, HEAD)),
        ],
        out_specs=[
            pl.BlockSpec((None, tm, d), lambda i, j: (jnp.minimum(j, N_ZF - 1), i, 0)),
            pl.BlockSpec((None, tm, d), lambda i, j: (jnp.maximum(j - N_ZF, 0), i, 0)),
            pl.BlockSpec((tm, HEAD), lambda i, j: (i, 0)),
        ],
        out_shape=[
            jax.ShapeDtypeStruct((N_ZF, n, d), F32),
            jax.ShapeDtypeStruct((N_ZB, n, d), BF16),
            jax.ShapeDtypeStruct((n, HEAD), F32),
        ],
        scratch_shapes=[pltpu.VMEM((tm, d), BF16)],
        compiler_params=_params("parallel", "arbitrary"),
        name="proj_in",
    )(x, g, w, wff, fb)


def _proj_cache_kernel(x_ref, g_ref, w_ref, fk_ref, fv_ref, dk_ref, dv_ref, zkv_ref):
    h = _rms(x_ref[...], g_ref[...]).astype(BF16)
    d = x_ref.shape[1]
    for s, ref in enumerate((fk_ref, fv_ref, dk_ref, dv_ref)):
        res = jnp.dot(h, w_ref[:, s * d:(s + 1) * d], preferred_element_type=F32)
        ref[...] = res.reshape(ref.shape)
        zkv_ref[s] = res.astype(BF16)


def _proj_cache(x, g, w, *, tm):
    n, d = x.shape
    heads = d // HEAD
    cache_spec = pl.BlockSpec((tm, heads, HEAD), lambda i: (i, 0, 0))
    cache_shape = jax.ShapeDtypeStruct((n, heads, HEAD), F32)
    return pl.pallas_call(
        _proj_cache_kernel,
        grid=(n // tm,),
        in_specs=[pl.BlockSpec((tm, d), lambda i: (i, 0)), _resident((1, d)), _resident(w.shape)],
        out_specs=[cache_spec, cache_spec, cache_spec, cache_spec,
                   pl.BlockSpec((N_ZS, tm, d), lambda i: (0, i, 0))],
        out_shape=[cache_shape, cache_shape, cache_shape, cache_shape,
                   jax.ShapeDtypeStruct((N_ZS, n, d), BF16)],
        compiler_params=_params("parallel"),
        name="proj_cache",
    )(x, g, w)


def _hgrn_kernel(zq_ref, zf_ref, zi_ref, zog_ref, lb_ref, gn_ref, s0_ref,
                 o_ref, sout_ref, st_ref, *, chunk, heads):
    t = pl.program_id(1)
    tt = zq_ref.shape[0]

    @pl.when(t == 0)
    def _():
        for h in range(heads):
            st_ref[h] = s0_ref[0, h].T

    lb = lb_ref[...]
    log_lb = jnp.log(lb)
    log_1m_lb = jnp.log1p(-lb)
    gn = gn_ref[...]
    row = lax.broadcasted_iota(jnp.int32, (chunk, chunk), 0)
    col = lax.broadcasted_iota(jnp.int32, (chunk, chunk), 1)
    causal = row >= col
    tril = causal.astype(F32)
    mid = chunk // 2

    def one_chunk(c, carry):
        r0 = pl.multiple_of(c * chunk, chunk)
        rows = pl.ds(r0, chunk)
        zf = zf_ref[rows, :]
        b = log_1m_lb + _log_sigmoid(zf)
        logf = jnp.maximum(log_lb, b) + _log1p_exp_neg(jnp.abs(log_lb - b))
        kk = (1.0 - lb) * _sigmoid(-zf)
        L = jnp.dot(tril, logf, preferred_element_type=F32, precision=lax.Precision.HIGHEST)
        for h in range(heads):
            sl = slice(h * HEAD, (h + 1) * HEAD)
            Lh = L[:, sl]
            Lr = Lh[mid:mid + 1, :]
            Lend = Lh[chunk - 1:chunk, :]
            q = zq_ref[rows, sl]
            k = kk[:, sl]
            v = zi_ref[rows, sl]
            q_in = (q * jnp.exp(Lh)).astype(BF16)
            q_r = (q * jnp.exp(Lh - Lr)).astype(BF16)
            k_r = (k * jnp.exp(Lr - Lh)).astype(BF16)
            k_end = (k * jnp.exp(Lend - Lh)).astype(BF16)
            a = lax.dot_general(q_r, k_r, _NT, preferred_element_type=F32)
            a = jnp.where(causal, a, 0.0).astype(BF16)
            st = st_ref[h]
            o = (lax.dot_general(q_in, st.astype(BF16), _NT, preferred_element_type=F32)
                 + jnp.dot(a, v, preferred_element_type=F32))
            st_ref[h] = st * jnp.exp(Lend) + lax.dot_general(v, k_end, _TN, preferred_element_type=F32)
            og = zog_ref[rows, sl].astype(F32)
            o_ref[rows, sl] = (_rms(o, gn) * (og * _sigmoid(og))).astype(o_ref.dtype)
        return carry

    nchunks = tt // chunk
    lax.fori_loop(0, nchunks, one_chunk, 0, unroll=2 if nchunks % 2 == 0 else 1)

    @pl.when(t == pl.num_programs(1) - 1)
    def _():
        for h in range(heads):
            sout_ref[0, h] = st_ref[h].T


def _hgrn(zf, zb, lb, gn, s0, *, batch, seq, chunk, tt):
    _, n, d = zf.shape
    heads = d // HEAD
    nt = seq // tt

    def zspec(slot):
        return pl.BlockSpec((None, tt, d), lambda b, t: (slot, b * nt + t, 0))

    state_spec = pl.BlockSpec((1, heads, HEAD, HEAD), lambda b, t: (b, 0, 0, 0))
    return pl.pallas_call(
        functools.partial(_hgrn_kernel, chunk=chunk, heads=heads),
        grid=(batch, nt),
        in_specs=[zspec(ZF_HQ), zspec(ZF_HF), zspec(ZB_HI), zspec(ZB_HOG),
                  _resident((1, d)), _resident((1, HEAD)), state_spec],
        out_specs=[pl.BlockSpec((tt, d), lambda b, t: (b * nt + t, 0)), state_spec],
        out_shape=[jax.ShapeDtypeStruct((n, d), BF16),
                   jax.ShapeDtypeStruct((batch, heads, HEAD, HEAD), F32)],
        scratch_shapes=[pltpu.VMEM((heads, HEAD, HEAD), F32)],
        compiler_params=_params("parallel", "arbitrary"),
        name="hgrn",
    )(zf, zf, zb, zb, lb, gn, s0)


def _cumsum_kernel(x_ref, o_ref, *, blk):
    nblk = x_ref.shape[0] // blk
    row = lax.broadcasted_iota(jnp.int32, (blk, blk), 0)
    col = lax.broadcasted_iota(jnp.int32, (blk, blk), 1)
    tril = (row >= col).astype(F32)

    def body(i, carry):
        rows = pl.ds(pl.multiple_of(i * blk, blk), blk)
        c = jnp.dot(tril, x_ref[rows, :], preferred_element_type=F32,
                    precision=lax.Precision.HIGHEST) + carry
        o_ref[rows, :] = c
        return c[blk - 1:blk, :]

    lax.fori_loop(0, nblk, body, jnp.zeros((1, x_ref.shape[1]), F32))


def _cumsum_time(x):
    return pl.pallas_call(
        functools.partial(_cumsum_kernel, blk=128),
        out_shape=jax.ShapeDtypeStruct(x.shape, F32),
        compiler_params=_params(),
        name="fox_cumsum",
    )(x)


STRIP = 64


def _lane_tiles(x, n):
    return x if n == 1 else jnp.concatenate([x] * n, axis=1)


def _online_softmax_block(c, q_aug, kt_blk, vblk, refs, fix_scores=None, aligned_diagonal=False):
    s_ref, p_ref, m_ref, acc_ref = refs
    tq, tk = s_ref.shape[1], s_ref.shape[2]
    strip = min(tq, STRIP)
    s_ref[c] = jnp.dot(q_aug, kt_blk, preferred_element_type=F32)
    for r0 in range(0, tq, strip):
        rs = slice(r0, r0 + strip)
        if aligned_diagonal:
            lo = r0 // HEAD * HEAD
            hi = (r0 + strip - 1) // HEAD * HEAD + HEAD
        else:
            lo, hi = 0, tk
        s = s_ref[c, rs, :hi]
        if fix_scores is not None:
            fixed = fix_scores(s[:, lo:], r0, strip, lo, hi - lo)
            s = fixed if lo == 0 else jnp.concatenate([s[:, :lo], fixed], axis=1)
        m_prev = m_ref[c, rs, :]
        m_new = jnp.maximum(m_prev, jnp.max(s, axis=1, keepdims=True))
        alpha = jnp.exp2(m_prev - m_new)
        p = jnp.exp2(s - _lane_tiles(m_new, hi // HEAD))
        m_ref[c, rs, :] = m_new
        acc_ref[c, rs, :] = _lane_tiles(alpha, 2) * acc_ref[c, rs, :]
        p_ref[c, rs, :hi] = p.astype(BF16)
        if hi < tk:
            p_ref[c, rs, hi:] = jnp.zeros((strip, tk - hi), BF16)
    acc_ref[c] += jnp.dot(p_ref[c], vblk, preferred_element_type=F32)


def _softmax_init(refs):
    _, _, m_ref, acc_ref = refs
    m_ref[...] = jnp.full(m_ref.shape, -jnp.inf, F32)
    acc_ref[...] = jnp.zeros(acc_ref.shape, F32)


def _softmax_result(refs, c):
    acc_ref = refs[3]
    return acc_ref[c, :, :HEAD] / acc_ref[c, :, HEAD:]


def _values_with_ones(v_ref, vnew_ref, vb_ref):
    rows, total = v_ref.shape[0], vb_ref.shape[0]
    new = 0 if vnew_ref is None else vnew_ref.shape[0]
    for hh in range(v_ref.shape[1] // HEAD):
        src = slice(hh * HEAD, (hh + 1) * HEAD)
        dst = slice(2 * hh * HEAD, (2 * hh + 1) * HEAD)
        vb_ref[0:rows, dst] = v_ref[:, src].astype(BF16)
        if new:
            vb_ref[rows:rows + new, dst] = vnew_ref[:, src].astype(BF16)
        if rows + new < total:
            vb_ref[rows + new:total, dst] = jnp.zeros((total - rows - new, HEAD), BF16)
        vb_ref[:, (2 * hh + 1) * HEAD:(2 * hh + 2) * HEAD] = jnp.ones((total, HEAD), BF16)


def _key_rows(k_ref, knew_ref, r0, start, cols):
    rows = k_ref.shape[0]
    if start + HEAD <= rows:
        return k_ref[pl.ds(r0 + start, HEAD), cols].astype(F32)
    assert knew_ref is not None and start >= rows and isinstance(r0, int) and r0 == 0
    off = start - rows
    take = max(0, min(HEAD, knew_ref.shape[0] - off))
    parts = []
    if take:
        parts.append(knew_ref[off:off + take, cols].astype(F32))
    if take < HEAD:
        parts.append(jnp.zeros((HEAD - take, HEAD), F32))
    return parts[0] if len(parts) == 1 else jnp.concatenate(parts, axis=0)


def _for_each_key_block(nkb, fill):
    if nkb == 1:
        fill(0, 0)
    else:
        lax.fori_loop(0, nkb, fill, 0)


def _softmax_scratch(chains, tq, tk):
    return [pltpu.VMEM((chains, tq, tk), F32), pltpu.VMEM((chains, tq, tk), BF16),
            pltpu.VMEM((chains, tq, HEAD), F32), pltpu.VMEM((chains, tq, 2 * HEAD), F32)]


def _loop_two_per_trip(lo, hi, body):
    pairs = (hi - lo) // 2

    def two(t, carry):
        j = lo + 2 * t
        body(j, 0)
        body(j + 1, 0)
        return carry

    lax.fori_loop(0, pairs, two, 0)
    tail = lo + 2 * pairs

    @pl.when(tail < hi)
    def _():
        body(tail, 0)


def _block_counts(past, i, tq, tk, k_valid):
    n_before = (past + i * tq) // tk
    kmax = jnp.minimum(((past + (i + 1) * tq - 1) // CHUNK + 1) * CHUNK, k_valid)
    return n_before, (kmax + tk - 1) // tk


def _split_new_rows(args, has_new):
    q_ref, k_ref, v_ref = args[:3]
    if has_new:
        return q_ref, k_ref, v_ref, args[3], args[4], args[5:]
    return q_ref, k_ref, v_ref, None, None, args[3:]


def _fox_kernel(*args, heads, past, k_valid, tk, has_new):
    q_ref, k_ref, v_ref, knew_ref, vnew_ref, rest = _split_new_rows(args, has_new)
    cq_ref, ck_ref, o_ref, kt_ref, vb_ref, stats_ref, *refs = rest
    b, hp, i = pl.program_id(0), pl.program_id(1), pl.program_id(2)
    tq = q_ref.shape[0]
    hps = q_ref.shape[1] // HEAD
    nkb = kt_ref.shape[1]
    sub = tk // HEAD

    blk = lax.broadcasted_iota(jnp.int32, (1, HEAD), 1)
    assert nkb <= HEAD

    @pl.when(i == 0)
    def _():
        _values_with_ones(v_ref, vnew_ref, vb_ref)
        stats_ref[...] = jnp.zeros(stats_ref.shape, F32)
        row = lax.broadcasted_iota(jnp.int32, (HEAD, tk), 0)
        for hh in range(hps):
            def fill(jb, carry, hh=hh):
                r0 = jb * tk if isinstance(jb, int) else pl.multiple_of(jb * tk, tk)
                per_key = jnp.zeros((1, HEAD), F32)
                for u in range(sub):
                    kblk = _key_rows(k_ref, knew_ref, r0, u * HEAD, slice(hh * HEAD, (hh + 1) * HEAD))
                    kt = kblk.T.astype(BF16)
                    kt_ref[hh, jb, 0:HEAD, u * HEAD:(u + 1) * HEAD] = kt
                    ktf = kt.astype(F32)
                    per_key = jnp.maximum(per_key, jnp.sum(ktf * ktf, axis=0, keepdims=True))
                norm2 = jnp.max(per_key, axis=1, keepdims=True)
                ck = LOG2E * ck_ref[hh, jb]
                kt_ref[hh, jb, HEAD:2 * HEAD, :] = _bias_lanes(row, _split3(-ck), True).astype(BF16)
                here = blk == jb
                stats_ref[hh, 0:1, :] = jnp.where(here, norm2, stats_ref[hh, 0:1, :])
                stats_ref[hh, 1:2, :] = jnp.where(here, jnp.min(ck, axis=1, keepdims=True),
                                                  stats_ref[hh, 1:2, :])
                return carry
            _for_each_key_block(nkb, fill)

    def first_needed_block(hh, cq, n_before):
        q32 = q_ref[:, hh * HEAD:(hh + 1) * HEAD].astype(F32)
        nq2 = jnp.max(jnp.sum(q32 * q32, axis=1, keepdims=True), axis=0, keepdims=True)
        reach = jnp.sqrt(nq2 * stats_ref[hh, 0:1, :])
        own = jnp.max(jnp.where(blk == n_before, reach, 0.0), axis=1, keepdims=True)
        upper = reach + jnp.max(cq, axis=0, keepdims=True) - stats_ref[hh, 1:2, :] + own
        needed = jnp.logical_or(jnp.logical_not(upper < -UNDERFLOW_LOG2), blk >= n_before)
        return jnp.min(jnp.where(needed, blk, HEAD))

    lane = lax.broadcasted_iota(jnp.int32, (tq, HEAD), 1)
    qpos = past + i * tq + lax.broadcasted_iota(jnp.int32, (tq, 1), 0)
    kiota = lax.broadcasted_iota(jnp.int32, (1, tk), 1)
    cq_tile = cq_ref[...]
    pair = lax.broadcasted_iota(jnp.int32, cq_tile.shape, 1)
    aligned = tq == tk and past % tk == 0
    n_before, nk = _block_counts(past, i, tq, tk, k_valid)
    q_aug = []
    j_start = n_before if aligned else 0
    for hh in range(hps):
        bh = b * heads + hp * hps + hh
        cq = LOG2E * jnp.sum(jnp.where(pair == bh, cq_tile, 0.0), axis=1, keepdims=True)
        aug = _bias_lanes(lane, _split3(cq), False).astype(BF16)
        q_aug.append(jnp.concatenate([q_ref[:, hh * HEAD:(hh + 1) * HEAD], aug], axis=1))
        if aligned:
            j_start = jnp.minimum(j_start, first_needed_block(hh, cq, n_before))

    def make_body(masked):
        def body(j, carry):
            rows = pl.ds(pl.multiple_of(j * tk, tk), tk)

            def causal(s, r0, strip, c0, cols):
                kpos = j * tk + kiota[:, c0:c0 + cols]
                return jnp.where(kpos <= qpos[r0:r0 + strip], s, -jnp.inf)

            for hh in range(hps):
                _online_softmax_block(hh, q_aug[hh], kt_ref[hh, j],
                                      vb_ref[rows, 2 * hh * HEAD:(2 * hh + 2) * HEAD], refs,
                                      causal if masked else None, masked and aligned)
            return carry
        return body

    _softmax_init(refs)
    _loop_two_per_trip(j_start, n_before, make_body(False))
    lax.fori_loop(n_before, nk, make_body(True), 0)
    for hh in range(hps):
        o_ref[:, hh * HEAD:(hh + 1) * HEAD] = _softmax_result(refs, hh).astype(o_ref.dtype)


def _kv_operands(k, v, new, batch, hps):
    arrays, specs = [], []
    for arr, slot in (k, v) + (() if new is None else ((new[0], new[1]), (new[0], new[2]))):
        rows = arr.shape[1] // batch
        arrays.append(arr)
        specs.append(pl.BlockSpec((None, rows, hps * HEAD), lambda b, h, i, slot=slot: (slot, b, h)))
    return arrays, specs


def _fox(zb, k, v, new, c_rows, c_cols, *, batch, heads, tq_total, tk_total, past, tq, tk):
    _, n, d = zb.shape
    hps = HEADS_PER_STEP
    nq = tq_total // tq
    nkb = tk_total // tk
    kv_arrays, kv_specs = _kv_operands(k, v, new, batch, hps)
    return pl.pallas_call(
        functools.partial(_fox_kernel, heads=heads, past=past, k_valid=past + tq_total, tk=tk,
                          has_new=new is not None),
        grid=(batch, heads // hps, nq),
        in_specs=[
            pl.BlockSpec((None, tq, hps * HEAD), lambda b, h, i: (ZB_FQ, b * nq + i, h)),
            *kv_specs,
            pl.BlockSpec((tq, batch * heads), lambda b, h, i: (past // tq + i, 0)),
            pl.BlockSpec((hps, nkb, 1, tk), lambda b, h, i: (b * (heads // hps) + h, 0, 0, 0)),
        ],
        out_specs=pl.BlockSpec((tq, hps * HEAD), lambda b, h, i: (b * nq + i, h)),
        out_shape=jax.ShapeDtypeStruct((n, d), BF16),
        scratch_shapes=[pltpu.VMEM((hps, nkb, 2 * HEAD, tk), BF16),
                        pltpu.VMEM((tk_total, 2 * hps * HEAD), BF16),
                        pltpu.VMEM((hps, 8, HEAD), F32)] + _softmax_scratch(hps, tq, tk),
        compiler_params=_params("parallel", "parallel", "arbitrary"),
        name="fox_attn",
    )(zb, *kv_arrays, c_rows, c_cols)


def _diff_kernel(*args, past, k_valid, tk, lam_init, has_new):
    q_ref, k_ref, v_ref, knew_ref, vnew_ref, rest = _split_new_rows(args, has_new)
    (slope_ref, lq1_ref, lk1_ref, lq2_ref, lk2_ref, gn_ref,
     o_ref, kt_ref, vb_ref, stats_ref, *refs) = rest
    i = pl.program_id(2)
    tq = q_ref.shape[0]
    hps = q_ref.shape[1] // HEAD
    nkb = kt_ref.shape[2]
    sub = tk // HEAD
    dq = HEAD // 2
    blk = lax.broadcasted_iota(jnp.int32, (1, HEAD), 1)
    assert nkb <= HEAD

    @pl.when(i == 0)
    def _():
        _values_with_ones(v_ref, vnew_ref, vb_ref)
        stats_ref[...] = jnp.zeros(stats_ref.shape, F32)
        row = lax.broadcasted_iota(jnp.int32, (HEAD, HEAD), 0)
        col = lax.broadcasted_iota(jnp.int32, (1, HEAD), 1)
        for hh in range(hps):
            slope2 = LOG2E * slope_ref[hh]

            def fill(jb, carry, hh=hh, slope2=slope2):
                r0 = jb * tk if isinstance(jb, int) else pl.multiple_of(jb * tk, tk)
                per_key = jnp.zeros((1, HEAD), F32)
                for u in range(sub):
                    kblk = _key_rows(k_ref, knew_ref, r0, u * HEAD, slice(hh * HEAD, (hh + 1) * HEAD))
                    kt = kblk.astype(BF16).astype(F32).T
                    per_key = jnp.maximum(per_key, jnp.sum(kt * kt, axis=0, keepdims=True))
                    kpos = (r0 + u * HEAD + col).astype(F32)
                    col_bias = _split3(slope2 * kpos)
                    lanes = slice(u * HEAD, (u + 1) * HEAD)
                    kt_ref[hh, 0, jb, :, lanes] = jnp.where(
                        row < dq, kt, _bias_lanes(row - dq, col_bias, True)).astype(BF16)
                    kt_ref[hh, 1, jb, :, lanes] = jnp.where(
                        row >= dq, kt, _bias_lanes(row, col_bias, True)).astype(BF16)
                stats_ref[hh, 0:1, :] = jnp.where(blk == jb, jnp.max(per_key, axis=1, keepdims=True),
                                                  stats_ref[hh, 0:1, :])
                return carry
            _for_each_key_block(nkb, fill)

    def first_needed_block(hh, slope2, n_before):
        q32 = q_ref[:, hh * HEAD:(hh + 1) * HEAD].astype(F32)
        nq2 = jnp.max(jnp.sum(q32 * q32, axis=1, keepdims=True), axis=0, keepdims=True)
        reach = jnp.sqrt(nq2 * stats_ref[hh, 0:1, :])
        own = jnp.max(jnp.where(blk == n_before, reach, 0.0), axis=1, keepdims=True)
        nearest = (past + i * tq - (blk + 1) * tk + 1).astype(F32)
        upper = reach - slope2 * nearest + own
        needed = jnp.logical_or(jnp.logical_not(upper < -UNDERFLOW_LOG2), blk >= n_before)
        return jnp.min(jnp.where(needed, blk, HEAD))

    lam = (jnp.exp(jnp.sum(lq1_ref[...] * lk1_ref[...], axis=1, keepdims=True))
           - jnp.exp(jnp.sum(lq2_ref[...] * lk2_ref[...], axis=1, keepdims=True)) + lam_init)

    lane = lax.broadcasted_iota(jnp.int32, (tq, HEAD), 1)
    qpos = past + i * tq + lax.broadcasted_iota(jnp.int32, (tq, 1), 0)
    qposf = qpos.astype(F32)
    qchunk = lax.shift_right_logical(qpos, CHUNK_SHIFT)
    kiota = lax.broadcasted_iota(jnp.int32, (1, tk), 1)
    aligned = tq == tk and past % tk == 0
    n_before, nk = _block_counts(past, i, tq, tk, k_valid)
    j_start = n_before if aligned else 0
    q_aug, slope_col = [], []
    for hh in range(hps):
        slope2 = LOG2E * slope_ref[hh][:, 0:1]
        slope_col.append(slope2)
        if aligned:
            j_start = jnp.minimum(j_start, first_needed_block(hh, slope2, n_before))
        row_bias = _split3(-slope2 * qposf)
        q = q_ref[:, hh * HEAD:(hh + 1) * HEAD].astype(F32)
        q1 = jnp.where(lane < dq, q, _bias_lanes(lane - dq, row_bias, False)).astype(BF16)
        q2 = jnp.where(lane >= dq, q, _bias_lanes(lane, row_bias, False)).astype(BF16)
        q_aug.append((q1, q2))

    def make_body(diagonal):
        def body(j, carry):
            rows = pl.ds(pl.multiple_of(j * tk, tk), tk)
            for hh in range(hps):
                vblk = vb_ref[rows, 2 * hh * HEAD:(2 * hh + 2) * HEAD]

                def fix(s, r0, strip, c0, cols, hh=hh):
                    kpos = j * tk + kiota[:, c0:c0 + cols]
                    ahead = jnp.maximum(kpos - qpos[r0:r0 + strip], 0).astype(F32)
                    visible = jnp.logical_and(
                        lax.shift_right_logical(kpos, CHUNK_SHIFT) <= qchunk[r0:r0 + strip],
                        kpos < k_valid)
                    return jnp.where(visible, s - (2.0 * slope_col[hh]) * ahead, -jnp.inf)

                for mp in range(2):
                    _online_softmax_block(2 * hh + mp, q_aug[hh][mp], kt_ref[hh, mp, j], vblk, refs,
                                          fix if diagonal else None, diagonal and aligned)
            return carry
        return body

    _softmax_init(refs)
    _loop_two_per_trip(j_start, n_before, make_body(False))
    lax.fori_loop(n_before, nk, make_body(True), 0)
    for hh in range(hps):
        o = _softmax_result(refs, 2 * hh) - lam * _softmax_result(refs, 2 * hh + 1)
        o_ref[:, hh * HEAD:(hh + 1) * HEAD] = (
            _rms(o, gn_ref[...]) * (1.0 - lam_init)).astype(o_ref.dtype)


def _diff(zb, k, v, new, slopes, lq1, lk1, lq2, lk2, gn, *, batch, heads, tq_total, tk_total, past,
          tq, tk, lam_init):
    _, n, d = zb.shape
    hps = HEADS_PER_STEP
    nq = tq_total // tq
    nkb = tk_total // tk
    kv_arrays, kv_specs = _kv_operands(k, v, new, batch, hps)
    lam_spec = _resident((1, HEAD // 2))
    return pl.pallas_call(
        functools.partial(_diff_kernel, past=past, k_valid=past + tq_total, tk=tk, lam_init=lam_init,
                          has_new=new is not None),
        grid=(batch, heads // hps, nq),
        in_specs=[
            pl.BlockSpec((None, tq, hps * HEAD), lambda b, h, i: (ZB_DQ, b * nq + i, h)),
            *kv_specs,
            pl.BlockSpec((hps, 1, HEAD), lambda b, h, i: (h, 0, 0)),
            lam_spec, lam_spec, lam_spec, lam_spec,
            _resident((1, HEAD)),
        ],
        out_specs=pl.BlockSpec((tq, hps * HEAD), lambda b, h, i: (b * nq + i, h)),
        out_shape=jax.ShapeDtypeStruct((n, d), BF16),
        scratch_shapes=[pltpu.VMEM((hps, 2, nkb, HEAD, tk), BF16),
                        pltpu.VMEM((tk_total, 2 * hps * HEAD), BF16),
                        pltpu.VMEM((hps, 8, HEAD), F32)] + _softmax_scratch(2 * hps, tq, tk),
        compiler_params=_params("parallel", "parallel", "arbitrary"),
        name="diff_attn",
    )(zb, *kv_arrays, slopes, lq1, lk1, lq2, lk2, gn)


def _merge_kernel(a1_ref, a2_ref, a3_ref, g1_ref, g2_ref, g3_ref, x_ref,
                  w1_ref, w2_ref, w3_ref, wo_ref, o_ref):
    def branch(a_ref, g_ref, w_ref):
        return _sigmoid(g_ref[...].astype(F32)) * jnp.dot(a_ref[...], w_ref[...],
                                                          preferred_element_type=F32)

    mix = branch(a1_ref, g1_ref, w1_ref) + branch(a2_ref, g2_ref, w2_ref) + branch(a3_ref, g3_ref, w3_ref)
    y = jnp.dot(mix.astype(BF16), wo_ref[...], preferred_element_type=F32)
    o_ref[...] = x_ref[...] + y


def _merge(a1, a2, a3, zb, x, w1, w2, w3, wo, *, tm):
    n, d = x.shape
    act = pl.BlockSpec((tm, d), lambda i: (i, 0))

    def gate(slot):
        return pl.BlockSpec((None, tm, d), lambda i: (slot, i, 0))

    wspec = _resident((d, d))
    return pl.pallas_call(
        _merge_kernel,
        grid=(n // tm,),
        in_specs=[act, act, act, gate(ZB_GHG), gate(ZB_GFOX), gate(ZB_GDIFF), act,
                  wspec, wspec, wspec, wspec],
        out_specs=act,
        out_shape=jax.ShapeDtypeStruct((n, d), F32),
        compiler_params=_params("parallel"),
        name="merge_out",
    )(a1, a2, a3, zb, zb, zb, x, w1, w2, w3, wo)


def _ffn_kernel(x_ref, g_ref, wgu_ref, wd_ref, gout_ref, o_ref, *, d_ff, fchunk, norm_out):
    x = x_ref[...]
    h = _rms(x, g_ref[...]).astype(BF16)
    acc = x
    for c0 in range(0, d_ff, fchunk):
        a = jnp.dot(h, wgu_ref[:, c0:c0 + fchunk], preferred_element_type=F32)
        b = jnp.dot(h, wgu_ref[:, d_ff + c0:d_ff + c0 + fchunk], preferred_element_type=F32)
        act = (a * _sigmoid(a) * b).astype(BF16)
        acc = acc + jnp.dot(act, wd_ref[c0:c0 + fchunk, :], preferred_element_type=F32)
    o_ref[...] = _rms(acc, gout_ref[...]) if norm_out else acc


def _ffn(x, g, wgu, wd, g_out, *, tm, norm_out):
    n, d = x.shape
    d_ff = wd.shape[0]
    fchunk = d_ff // 2
    assert fchunk % HEAD == 0
    act = pl.BlockSpec((tm, d), lambda i: (i, 0))
    return pl.pallas_call(
        functools.partial(_ffn_kernel, d_ff=d_ff, fchunk=fchunk, norm_out=norm_out),
        grid=(n // tm,),
        in_specs=[act, _resident((1, d)), _resident(wgu.shape), _resident(wd.shape), _resident((1, d))],
        out_specs=act,
        out_shape=jax.ShapeDtypeStruct((n, d), F32),
        compiler_params=_params("parallel"),
        name="ffn",
    )(x, g, wgu, wd, g_out)


def _prep_layer_weights(w_in_l, d):
    hd = HEADS
    splits = (d, d, d, d, d, d, d, hd, d, d, d, d, d, d)
    offs = [0]
    for s in splits:
        offs.append(offs[-1] + s)
    cols = [w_in_l[:, offs[k]:offs[k + 1]] for k in range(len(splits))]
    hq, hf, hi, hog, fq, fk, fv, ff, dq, dk, dv, g_hg, g_fox, g_diff = cols
    fq = fq * (HEAD ** -0.5 * LOG2E)
    dq = dq * ((HEAD // 2) ** -0.5 * LOG2E)
    w = jnp.concatenate([hq, hf, hi, hog, fq, dq, g_hg, g_fox, g_diff], axis=1).astype(BF16)
    w_cache = jnp.concatenate([fk, fv, dk, dv], axis=1).astype(BF16)
    wff = jnp.pad(ff, ((0, 0), (0, HEAD - hd))).astype(BF16)
    return w, w_cache, wff


def _pick_tile(n, want):
    t = min(n, want)
    assert n % t == 0
    return t


def _trunk(x, past, wts, lb_all, slopes):
    batch, seq, d = x.shape
    n = batch * seq
    heads = d // HEAD
    bh = batch * heads
    depth = len(wts)
    prompt = past is None
    plen = 0 if prompt else past[0].shape[2]
    tm = _pick_tile(n, 512)
    tm_proj = _pick_tile(n, 1024)

    if prompt:
        chunk, tt = CHUNK, _pick_tile(seq, 512)
        tq = tk = _pick_tile(seq, 512)
        tk_total = seq
    else:
        chunk, tt = seq, seq
        tq = seq
        tk = tk_total = -(-(plen + seq) // HEAD) * HEAD
    assert plen % tq == 0
    pad = tk_total - (plen + seq)

    xf = x.reshape(n, d)
    states = []
    for l in range(depth):
        w = wts[l]
        lam_init = 0.8 - 0.6 * math.exp(-0.3 * l)
        zf, zb, fl = _proj_in(xf, w["norm_mix"], w["w_in"], w["w_ff"], w["f_bias"], tm=tm_proj)
        fk, fv, dk, dv, zs = _proj_cache(xf, w["norm_mix"], w["w_cache"], tm=tm)
        fox_logf = fl[:, :heads].reshape(batch, seq, heads)

        if prompt:
            s0 = jnp.zeros((batch, heads, HEAD, HEAD), F32)
        else:
            s0 = past[5][l]
        hg_o, hg_s = _hgrn(zf, zb, lb_all[l][None, :], w["hg_norm"], s0,
                           batch=batch, seq=seq, chunk=chunk, tt=tt)

        if prompt:
            fox_k, fox_v, fox_new = (zs, ZS_FK), (zs, ZS_FV), None
            diff_k, diff_v, diff_new = (zs, ZS_DK), (zs, ZS_DV), None
            logf_all = fox_logf
        else:
            def cached(cache):
                return cache.reshape(1, batch * plen, d), 0
            fox_k, fox_v, fox_new = cached(past[0][l]), cached(past[1][l]), (zs, ZS_FK, ZS_FV)
            diff_k, diff_v, diff_new = cached(past[3][l]), cached(past[4][l]), (zs, ZS_DK, ZS_DV)
            logf_all = jnp.pad(jnp.concatenate([past[2][l], fox_logf], axis=1),
                               ((0, 0), (0, pad), (0, 0)))
        c_rows = _cumsum_time(logf_all.transpose(1, 0, 2).reshape(tk_total, bh))
        c_cols = c_rows.T.reshape(bh, tk_total // tk, 1, tk)
        fox_o = _fox(zb, fox_k, fox_v, fox_new, c_rows, c_cols, batch=batch, heads=heads,
                     tq_total=seq, tk_total=tk_total, past=plen, tq=tq, tk=tk)

        diff_o = _diff(zb, diff_k, diff_v, diff_new, slopes, w["lq1"], w["lk1"], w["lq2"], w["lk2"],
                       w["diff_norm"], batch=batch, heads=heads, tq_total=seq, tk_total=tk_total,
                       past=plen, tq=tq, tk=tk, lam_init=lam_init)

        xf = _merge(hg_o, fox_o, diff_o, zb, xf, w["w_br_hgrn"], w["w_br_fox"], w["w_br_diff"],
                    w["w_out"], tm=tm)
        xf = _ffn(xf, w["norm_ffn"], w["w_gate_up"], w["w_down"], w["norm_final"], tm=tm,
                  norm_out=l == depth - 1)

        def heads4(a):
            return a.reshape(batch, seq, heads, HEAD)
        states.append((heads4(fk), heads4(fv), fox_logf, heads4(dk), heads4(dv), hg_s))

    stacked = tuple(jnp.stack(s, axis=0) for s in zip(*states))
    return xf.reshape(batch, seq, d), stacked


def kernel(x_prompt, x_sample, cache_fox_k, cache_fox_v, cache_fox_logf, cache_diff_k, cache_diff_v,
           state_hgrn, norm_mix, w_in, hg_lb_logits, hg_norm, fox_f_bias, diff_lam_q1, diff_lam_k1,
           diff_lam_q2, diff_lam_k2, diff_norm, w_br_hgrn, w_br_fox, w_br_diff, w_out, norm_ffn,
           w_gate_up, w_down, norm_final):
    depth, d, _ = w_in.shape
    wts = []
    for l in range(depth):
        w, w_cache, wff = _prep_layer_weights(w_in[l], d)
        wts.append(dict(
            norm_mix=norm_mix[l][None, :], w_in=w, w_cache=w_cache, w_ff=wff,
            f_bias=jnp.pad(fox_f_bias[l], (0, HEAD - HEADS))[None, :],
            hg_norm=hg_norm[l][None, :],
            lq1=diff_lam_q1[l][None, :], lk1=diff_lam_k1[l][None, :],
            lq2=diff_lam_q2[l][None, :], lk2=diff_lam_k2[l][None, :],
            diff_norm=diff_norm[l][None, :],
            w_br_hgrn=w_br_hgrn[l].astype(BF16), w_br_fox=w_br_fox[l].astype(BF16),
            w_br_diff=w_br_diff[l].astype(BF16), w_out=w_out[l].astype(BF16),
            norm_ffn=norm_ffn[l][None, :],
            w_gate_up=w_gate_up[l].astype(BF16), w_down=w_down[l].astype(BF16),
            norm_final=norm_final[None, :],
        ))
    cs = jnp.cumsum(jax.nn.softmax(hg_lb_logits.astype(F32), axis=0), axis=0)
    lb_all = cs - cs[0:1]
    slope = jnp.exp2(-8.0 * jnp.arange(1, HEADS + 1, dtype=F32) / HEADS)
    slopes = jnp.broadcast_to(slope[:, None, None], (HEADS, 1, HEAD))

    y_p, st_p = _trunk(x_prompt, None, wts, lb_all, slopes)
    past = (cache_fox_k, cache_fox_v, cache_fox_logf, cache_diff_k, cache_diff_v, state_hgrn)
    y_s, st_s = _trunk(x_sample, past, wts, lb_all, slopes)
    return (y_p, y_s) + st_p + st_s
```

```python
import functools
import math

import jax
import jax.numpy as jnp
from jax import lax
from jax.experimental import pallas as pl
from jax.experimental.pallas import tpu as pltpu

EPS = 1e-6
CHUNK = 64
CHUNK_SHIFT = 6
assert 1 << CHUNK_SHIFT == CHUNK
HEAD = 128
HEADS = 8
HEADS_PER_STEP = 4
LOG2E = math.log2(math.e)
UNDERFLOW_LOG2 = 160.0

ZF_HQ, ZF_HF = range(2)
ZB_HI, ZB_HOG, ZB_FQ, ZB_DQ, ZB_GHG, ZB_GFOX, ZB_GDIFF = range(7)
ZS_FK, ZS_FV, ZS_DK, ZS_DV = range(4)
N_ZF, N_ZB, N_ZS = 2, 7, 4

V7X_VMEM_LIMIT_BYTES = 60 * 1024 * 1024

_NT = (((1,), (1,)), ((), ()))
_TN = (((0,), (0,)), ((), ()))
BF16 = jnp.bfloat16
F32 = jnp.float32


def _params(*semantics):
    return pltpu.CompilerParams(dimension_semantics=semantics,
                                vmem_limit_bytes=V7X_VMEM_LIMIT_BYTES)


def _sigmoid(x):
    return 1.0 / (1.0 + jnp.exp(-x))


def _log1p_exp_neg(a):
    return jnp.log(1.0 + jnp.exp(-a))


def _log_sigmoid(x):
    return jnp.minimum(x, 0.0) - _log1p_exp_neg(jnp.abs(x))


def _rms(x, g):
    var = jnp.mean(x * x, axis=-1, keepdims=True)
    return (x * lax.rsqrt(var + EPS)) * g


def _resident(shape):
    nd = len(shape)
    return pl.BlockSpec(shape, lambda *_: (0,) * nd)


def _split3(x):
    hi = x.astype(BF16).astype(F32)
    r = x - hi
    mid = r.astype(BF16).astype(F32)
    lo = (r - mid).astype(BF16).astype(F32)
    return hi, mid, lo


def _bias_lanes(pos, row_terms, one_lanes_first):
    t0, t1, t2 = row_terms
    base = 3 if one_lanes_first else 0
    ones_lo = 0 if one_lanes_first else 3
    out = jnp.where(pos == base, t0, jnp.where(pos == base + 1, t1, jnp.where(pos == base + 2, t2, 0.0)))
    return jnp.where(jnp.logical_and(pos >= ones_lo, pos < ones_lo + 3), 1.0, out)


def _proj_in_kernel(x_ref, g_ref, w_ref, wff_ref, fb_ref, zf_ref, zb_ref, fl_ref, h_ref):
    j = pl.program_id(1)

    @pl.when(j == 0)
    def _():
        h = _rms(x_ref[...], g_ref[...]).astype(BF16)
        h_ref[...] = h
        ff = jnp.dot(h, wff_ref[...], preferred_element_type=F32)
        fl_ref[...] = _log_sigmoid(ff + fb_ref[...])

    res = jnp.dot(h_ref[...], w_ref[...], preferred_element_type=F32)

    @pl.when(j < N_ZF)
    def _():
        zf_ref[...] = res

    @pl.when(j >= N_ZF)
    def _():
        zb_ref[...] = res.astype(BF16)


def _proj_in(x, g, w, wff, fb, *, tm):
    n, d = x.shape
    ngroups = N_ZF + N_ZB
    return pl.pallas_call(
        _proj_in_kernel,
        grid=(n // tm, ngroups),
        in_specs=[
            pl.BlockSpec((tm, d), lambda i, j: (i, 0)),
            _resident((1, d)),
            pl.BlockSpec((d, d), lambda i, j: (0, j)),
            _resident((d, HEAD)),
            _resident((1, HEAD)),
        ],
        out_specs=[
            pl.BlockSpec((None, tm, d), lambda i, j: (jnp.minimum(j, N_ZF - 1), i, 0)),
            pl.BlockSpec((None, tm, d), lambda i, j: (jnp.maximum(j - N_ZF, 0), i, 0)),
            pl.BlockSpec((tm, HEAD), lambda i, j: (i, 0)),
        ],
        out_shape=[
            jax.ShapeDtypeStruct((N_ZF, n, d), F32),
            jax.ShapeDtypeStruct((N_ZB, n, d), BF16),
            jax.ShapeDtypeStruct((n, HEAD), F32),
        ],
        scratch_shapes=[pltpu.VMEM((tm, d), BF16)],
        compiler_params=_params("parallel", "arbitrary"),
        name="proj_in",
    )(x, g, w, wff, fb)


def _proj_cache_kernel(x_ref, g_ref, w_ref, fk_ref, fv_ref, dk_ref, dv_ref, zkv_ref):
    h = _rms(x_ref[...], g_ref[...]).astype(BF16)
    d = x_ref.shape[1]
    for s, ref in enumerate((fk_ref, fv_ref, dk_ref, dv_ref)):
        res = jnp.dot(h, w_ref[:, s * d:(s + 1) * d], preferred_element_type=F32)
        ref[...] = res.reshape(ref.shape)
        zkv_ref[s] = res.astype(BF16)


def _proj_cache(x, g, w, *, tm):
    n, d = x.shape
    heads = d // HEAD
    cache_spec = pl.BlockSpec((tm, heads, HEAD), lambda i: (i, 0, 0))
    cache_shape = jax.ShapeDtypeStruct((n, heads, HEAD), F32)
    return pl.pallas_call(
        _proj_cache_kernel,
        grid=(n // tm,),
        in_specs=[pl.BlockSpec((tm, d), lambda i: (i, 0)), _resident((1, d)), _resident(w.shape)],
        out_specs=[cache_spec, cache_spec, cache_spec, cache_spec,
                   pl.BlockSpec((N_ZS, tm, d), lambda i: (0, i, 0))],
        out_shape=[cache_shape, cache_shape, cache_shape, cache_shape,
                   jax.ShapeDtypeStruct((N_ZS, n, d), BF16)],
        compiler_params=_params("parallel"),
        name="proj_cache",
    )(x, g, w)


def _hgrn_kernel(zq_ref, zf_ref, zi_ref, zog_ref, lb_ref, gn_ref, s0_ref,
                 o_ref, sout_ref, st_ref, *, chunk, heads):
    t = pl.program_id(1)
    tt = zq_ref.shape[0]

    @pl.when(t == 0)
    def _():
        for h in range(heads):
            st_ref[h] = s0_ref[0, h].T

    lb = lb_ref[...]
    log_lb = jnp.log(lb)
    log_1m_lb = jnp.log1p(-lb)
    gn = gn_ref[...]
    row = lax.broadcasted_iota(jnp.int32, (chunk, chunk), 0)
    col = lax.broadcasted_iota(jnp.int32, (chunk, chunk), 1)
    causal = row >= col
    tril = causal.astype(F32)
    mid = chunk // 2

    def one_chunk(c, carry):
        r0 = pl.multiple_of(c * chunk, chunk)
        rows = pl.ds(r0, chunk)
        zf = zf_ref[rows, :]
        b = log_1m_lb + _log_sigmoid(zf)
        logf = jnp.maximum(log_lb, b) + _log1p_exp_neg(jnp.abs(log_lb - b))
        kk = (1.0 - lb) * _sigmoid(-zf)
        L = jnp.dot(tril, logf, preferred_element_type=F32, precision=lax.Precision.HIGHEST)
        for h in range(heads):
            sl = slice(h * HEAD, (h + 1) * HEAD)
            Lh = L[:, sl]
            Lr = Lh[mid:mid + 1, :]
            Lend = Lh[chunk - 1:chunk, :]
            q = zq_ref[rows, sl]
            k = kk[:, sl]
            v = zi_ref[rows, sl]
            q_in = (q * jnp.exp(Lh)).astype(BF16)
            q_r = (q * jnp.exp(Lh - Lr)).astype(BF16)
            k_r = (k * jnp.exp(Lr - Lh)).astype(BF16)
            k_end = (k * jnp.exp(Lend - Lh)).astype(BF16)
            a = lax.dot_general(q_r, k_r, _NT, preferred_element_type=F32)
            a = jnp.where(causal, a, 0.0).astype(BF16)
            st = st_ref[h]
            o = (lax.dot_general(q_in, st.astype(BF16), _NT, preferred_element_type=F32)
                 + jnp.dot(a, v, preferred_element_type=F32))
            st_ref[h] = st * jnp.exp(Lend) + lax.dot_general(v, k_end, _TN, preferred_element_type=F32)
            og = zog_ref[rows, sl].astype(F32)
            o_ref[rows, sl] = (_rms(o, gn) * (og * _sigmoid(og))).astype(o_ref.dtype)
        return carry

    nchunks = tt // chunk
    lax.fori_loop(0, nchunks, one_chunk, 0, unroll=2 if nchunks % 2 == 0 else 1)

    @pl.when(t == pl.num_programs(1) - 1)
    def _():
        for h in range(heads):
            sout_ref[0, h] = st_ref[h].T


def _hgrn(zf, zb, lb, gn, s0, *, batch, seq, chunk, tt):
    _, n, d = zf.shape
    heads = d // HEAD
    nt = seq // tt

    def zspec(slot):
        return pl.BlockSpec((None, tt, d), lambda b, t: (slot, b * nt + t, 0))

    state_spec = pl.BlockSpec((1, heads, HEAD, HEAD), lambda b, t: (b, 0, 0, 0))
    return pl.pallas_call(
        functools.partial(_hgrn_kernel, chunk=chunk, heads=heads),
        grid=(batch, nt),
        in_specs=[zspec(ZF_HQ), zspec(ZF_HF), zspec(ZB_HI), zspec(ZB_HOG),
                  _resident((1, d)), _resident((1, HEAD)), state_spec],
        out_specs=[pl.BlockSpec((tt, d), lambda b, t: (b * nt + t, 0)), state_spec],
        out_shape=[jax.ShapeDtypeStruct((n, d), BF16),
                   jax.ShapeDtypeStruct((batch, heads, HEAD, HEAD), F32)],
        scratch_shapes=[pltpu.VMEM((heads, HEAD, HEAD), F32)],
        compiler_params=_params("parallel", "arbitrary"),
        name="hgrn",
    )(zf, zf, zb, zb, lb, gn, s0)


def _cumsum_kernel(x_ref, o_ref, *, blk):
    nblk = x_ref.shape[0] // blk
    row = lax.broadcasted_iota(jnp.int32, (blk, blk), 0)
    col = lax.broadcasted_iota(jnp.int32, (blk, blk), 1)
    tril = (row >= col).astype(F32)

    def body(i, carry):
        rows = pl.ds(pl.multiple_of(i * blk, blk), blk)
        c = jnp.dot(tril, x_ref[rows, :], preferred_element_type=F32,
                    precision=lax.Precision.HIGHEST) + carry
        o_ref[rows, :] = c
        return c[blk - 1:blk, :]

    lax.fori_loop(0, nblk, body, jnp.zeros((1, x_ref.shape[1]), F32))


def _cumsum_time(x):
    return pl.pallas_call(
        functools.partial(_cumsum_kernel, blk=128),
        out_shape=jax.ShapeDtypeStruct(x.shape, F32),
        compiler_params=_params(),
        name="fox_cumsum",
    )(x)


STRIP = 64


def _lane_tiles(x, n):
    return x if n == 1 else jnp.concatenate([x] * n, axis=1)


def _online_softmax_block(c, q_aug, kt_blk, vblk, refs, fix_scores=None, aligned_diagonal=False):
    s_ref, p_ref, m_ref, acc_ref = refs
    tq, tk = s_ref.shape[1], s_ref.shape[2]
    strip = min(tq, STRIP)
    s_ref[c] = jnp.dot(q_aug, kt_blk, preferred_element_type=F32)
    for r0 in range(0, tq, strip):
        rs = slice(r0, r0 + strip)
        if aligned_diagonal:
            lo = r0 // HEAD * HEAD
            hi = (r0 + strip - 1) // HEAD * HEAD + HEAD
        else:
            lo, hi = 0, tk
        s = s_ref[c, rs, :hi]
        if fix_scores is not None:
            fixed = fix_scores(s[:, lo:], r0, strip, lo, hi - lo)
            s = fixed if lo == 0 else jnp.concatenate([s[:, :lo], fixed], axis=1)
        m_prev = m_ref[c, rs, :]
        m_new = jnp.maximum(m_prev, jnp.max(s, axis=1, keepdims=True))
        alpha = jnp.exp2(m_prev - m_new)
        p = jnp.exp2(s - _lane_tiles(m_new, hi // HEAD))
        m_ref[c, rs, :] = m_new
        acc_ref[c, rs, :] = _lane_tiles(alpha, 2) * acc_ref[c, rs, :]
        p_ref[c, rs, :hi] = p.astype(BF16)
        if hi < tk:
            p_ref[c, rs, hi:] = jnp.zeros((strip, tk - hi), BF16)
    acc_ref[c] += jnp.dot(p_ref[c], vblk, preferred_element_type=F32)


def _softmax_init(refs):
    _, _, m_ref, acc_ref = refs
    m_ref[...] = jnp.full(m_ref.shape, -jnp.inf, F32)
    acc_ref[...] = jnp.zeros(acc_ref.shape, F32)


def _softmax_result(refs, c):
    acc_ref = refs[3]
    return acc_ref[c, :, :HEAD] / acc_ref[c, :, HEAD:]


def _values_with_ones(v_ref, vnew_ref, vb_ref):
    rows, total = v_ref.shape[0], vb_ref.shape[0]
    new = 0 if vnew_ref is None else vnew_ref.shape[0]
    for hh in range(v_ref.shape[1] // HEAD):
        src = slice(hh * HEAD, (hh + 1) * HEAD)
        dst = slice(2 * hh * HEAD, (2 * hh + 1) * HEAD)
        vb_ref[0:rows, dst] = v_ref[:, src].astype(BF16)
        if new:
            vb_ref[rows:rows + new, dst] = vnew_ref[:, src].astype(BF16)
        if rows + new < total:
            vb_ref[rows + new:total, dst] = jnp.zeros((total - rows - new, HEAD), BF16)
        vb_ref[:, (2 * hh + 1) * HEAD:(2 * hh + 2) * HEAD] = jnp.ones((total, HEAD), BF16)


def _key_rows(k_ref, knew_ref, r0, start, cols):
    rows = k_ref.shape[0]
    if start + HEAD <= rows:
        return k_ref[pl.ds(r0 + start, HEAD), cols].astype(F32)
    assert knew_ref is not None and start >= rows and isinstance(r0, int) and r0 == 0
    off = start - rows
    take = max(0, min(HEAD, knew_ref.shape[0] - off))
    parts = []
    if take:
        parts.append(knew_ref[off:off + take, cols].astype(F32))
    if take < HEAD:
        parts.append(jnp.zeros((HEAD - take, HEAD), F32))
    return parts[0] if len(parts) == 1 else jnp.concatenate(parts, axis=0)


def _for_each_key_block(nkb, fill):
    if nkb == 1:
        fill(0, 0)
    else:
        lax.fori_loop(0, nkb, fill, 0)


def _softmax_scratch(chains, tq, tk):
    return [pltpu.VMEM((chains, tq, tk), F32), pltpu.VMEM((chains, tq, tk), BF16),
            pltpu.VMEM((chains, tq, HEAD), F32), pltpu.VMEM((chains, tq, 2 * HEAD), F32)]


def _loop_two_per_trip(lo, hi, body):
    pairs = (hi - lo) // 2

    def two(t, carry):
        j = lo + 2 * t
        body(j, 0)
        body(j + 1, 0)
        return carry

    lax.fori_loop(0, pairs, two, 0)
    tail = lo + 2 * pairs

    @pl.when(tail < hi)
    def _():
        body(tail, 0)


def _block_counts(past, i, tq, tk, k_valid):
    n_before = (past + i * tq) // tk
    kmax = jnp.minimum(((past + (i + 1) * tq - 1) // CHUNK + 1) * CHUNK, k_valid)
    return n_before, (kmax + tk - 1) // tk


def _split_new_rows(args, has_new):
    q_ref, k_ref, v_ref = args[:3]
    if has_new:
        return q_ref, k_ref, v_ref, args[3], args[4], args[5:]
    return q_ref, k_ref, v_ref, None, None, args[3:]


def _fox_kernel(*args, heads, past, k_valid, tk, has_new):
    q_ref, k_ref, v_ref, knew_ref, vnew_ref, rest = _split_new_rows(args, has_new)
    cq_ref, ck_ref, o_ref, kt_ref, vb_ref, stats_ref, *refs = rest
    b, hp, i = pl.program_id(0), pl.program_id(1), pl.program_id(2)
    tq = q_ref.shape[0]
    hps = q_ref.shape[1] // HEAD
    nkb = kt_ref.shape[1]
    sub = tk // HEAD

    blk = lax.broadcasted_iota(jnp.int32, (1, HEAD), 1)
    assert nkb <= HEAD

    @pl.when(i == 0)
    def _():
        _values_with_ones(v_ref, vnew_ref, vb_ref)
        stats_ref[...] = jnp.zeros(stats_ref.shape, F32)
        row = lax.broadcasted_iota(jnp.int32, (HEAD, tk), 0)
        for hh in range(hps):
            def fill(jb, carry, hh=hh):
                r0 = jb * tk if isinstance(jb, int) else pl.multiple_of(jb * tk, tk)
                per_key = jnp.zeros((1, HEAD), F32)
                for u in range(sub):
                    kblk = _key_rows(k_ref, knew_ref, r0, u * HEAD, slice(hh * HEAD, (hh + 1) * HEAD))
                    kt = kblk.T.astype(BF16)
                    kt_ref[hh, jb, 0:HEAD, u * HEAD:(u + 1) * HEAD] = kt
                    ktf = kt.astype(F32)
                    per_key = jnp.maximum(per_key, jnp.sum(ktf * ktf, axis=0, keepdims=True))
                norm2 = jnp.max(per_key, axis=1, keepdims=True)
                ck = LOG2E * ck_ref[hh, jb]
                kt_ref[hh, jb, HEAD:2 * HEAD, :] = _bias_lanes(row, _split3(-ck), True).astype(BF16)
                here = blk == jb
                stats_ref[hh, 0:1, :] = jnp.where(here, norm2, stats_ref[hh, 0:1, :])
                stats_ref[hh, 1:2, :] = jnp.where(here, jnp.min(ck, axis=1, keepdims=True),
                                                  stats_ref[hh, 1:2, :])
                return carry
            _for_each_key_block(nkb, fill)

    def first_needed_block(hh, cq, n_before):
        q32 = q_ref[:, hh * HEAD:(hh + 1) * HEAD].astype(F32)
        nq2 = jnp.max(jnp.sum(q32 * q32, axis=1, keepdims=True), axis=0, keepdims=True)
        reach = jnp.sqrt(nq2 * stats_ref[hh, 0:1, :])
        own = jnp.max(jnp.where(blk == n_before, reach, 0.0), axis=1, keepdims=True)
        upper = reach + jnp.max(cq, axis=0, keepdims=True) - stats_ref[hh, 1:2, :] + own
        needed = jnp.logical_or(jnp.logical_not(upper < -UNDERFLOW_LOG2), blk >= n_before)
        return jnp.min(jnp.where(needed, blk, HEAD))

    lane = lax.broadcasted_iota(jnp.int32, (tq, HEAD), 1)
    qpos = past + i * tq + lax.broadcasted_iota(jnp.int32, (tq, 1), 0)
    kiota = lax.broadcasted_iota(jnp.int32, (1, tk), 1)
    cq_tile = cq_ref[...]
    pair = lax.broadcasted_iota(jnp.int32, cq_tile.shape, 1)
    aligned = tq == tk and past % tk == 0
    n_before, nk = _block_counts(past, i, tq, tk, k_valid)
    q_aug = []
    j_start = n_before if aligned else 0
    for hh in range(hps):
        bh = b * heads + hp * hps + hh
        cq = LOG2E * jnp.sum(jnp.where(pair == bh, cq_tile, 0.0), axis=1, keepdims=True)
        aug = _bias_lanes(lane, _split3(cq), False).astype(BF16)
        q_aug.append(jnp.concatenate([q_ref[:, hh * HEAD:(hh + 1) * HEAD], aug], axis=1))
        if aligned:
            j_start = jnp.minimum(j_start, first_needed_block(hh, cq, n_before))

    def make_body(masked):
        def body(j, carry):
            rows = pl.ds(pl.multiple_of(j * tk, tk), tk)

            def causal(s, r0, strip, c0, cols):
                kpos = j * tk + kiota[:, c0:c0 + cols]
                return jnp.where(kpos <= qpos[r0:r0 + strip], s, -jnp.inf)

            for hh in range(hps):
                _online_softmax_block(hh, q_aug[hh], kt_ref[hh, j],
                                      vb_ref[rows, 2 * hh * HEAD:(2 * hh + 2) * HEAD], refs,
                                      causal if masked else None, masked and aligned)
            return carry
        return body

    _softmax_init(refs)
    _loop_two_per_trip(j_start, n_before, make_body(False))
    lax.fori_loop(n_before, nk, make_body(True), 0)
    for hh in range(hps):
        o_ref[:, hh * HEAD:(hh + 1) * HEAD] = _softmax_result(refs, hh).astype(o_ref.dtype)


def _kv_operands(k, v, new, batch, hps):
    arrays, specs = [], []
    for arr, slot in (k, v) + (() if new is None else ((new[0], new[1]), (new[0], new[2]))):
        rows = arr.shape[1] // batch
        arrays.append(arr)
        specs.append(pl.BlockSpec((None, rows, hps * HEAD), lambda b, h, i, slot=slot: (slot, b, h)))
    return arrays, specs


def _fox(zb, k, v, new, c_rows, c_cols, *, batch, heads, tq_total, tk_total, past, tq, tk):
    _, n, d = zb.shape
    hps = HEADS_PER_STEP
    nq = tq_total // tq
    nkb = tk_total // tk
    kv_arrays, kv_specs = _kv_operands(k, v, new, batch, hps)
    return pl.pallas_call(
        functools.partial(_fox_kernel, heads=heads, past=past, k_valid=past + tq_total, tk=tk,
                          has_new=new is not None),
        grid=(batch, heads // hps, nq),
        in_specs=[
            pl.BlockSpec((None, tq, hps * HEAD), lambda b, h, i: (ZB_FQ, b * nq + i, h)),
            *kv_specs,
            pl.BlockSpec((tq, batch * heads), lambda b, h, i: (past // tq + i, 0)),
            pl.BlockSpec((hps, nkb, 1, tk), lambda b, h, i: (b * (heads // hps) + h, 0, 0, 0)),
        ],
        out_specs=pl.BlockSpec((tq, hps * HEAD), lambda b, h, i: (b * nq + i, h)),
        out_shape=jax.ShapeDtypeStruct((n, d), BF16),
        scratch_shapes=[pltpu.VMEM((hps, nkb, 2 * HEAD, tk), BF16),
                        pltpu.VMEM((tk_total, 2 * hps * HEAD), BF16),
                        pltpu.VMEM((hps, 8, HEAD), F32)] + _softmax_scratch(hps, tq, tk),
        compiler_params=_params("parallel", "parallel", "arbitrary"),
        name="fox_attn",
    )(zb, *kv_arrays, c_rows, c_cols)


def _diff_kernel(*args, past, k_valid, tk, lam_init, has_new):
    q_ref, k_ref, v_ref, knew_ref, vnew_ref, rest = _split_new_rows(args, has_new)
    (slope_ref, lq1_ref, lk1_ref, lq2_ref, lk2_ref, gn_ref,
     o_ref, kt_ref, vb_ref, stats_ref, *refs) = rest
    i = pl.program_id(2)
    tq = q_ref.shape[0]
    hps = q_ref.shape[1] // HEAD
    nkb = kt_ref.shape[2]
    sub = tk // HEAD
    dq = HEAD // 2
    blk = lax.broadcasted_iota(jnp.int32, (1, HEAD), 1)
    assert nkb <= HEAD

    @pl.when(i == 0)
    def _():
        _values_with_ones(v_ref, vnew_ref, vb_ref)
        stats_ref[...] = jnp.zeros(stats_ref.shape, F32)
        row = lax.broadcasted_iota(jnp.int32, (HEAD, HEAD), 0)
        col = lax.broadcasted_iota(jnp.int32, (1, HEAD), 1)
        for hh in range(hps):
            slope2 = LOG2E * slope_ref[hh]

            def fill(jb, carry, hh=hh, slope2=slope2):
                r0 = jb * tk if isinstance(jb, int) else pl.multiple_of(jb * tk, tk)
                per_key = jnp.zeros((1, HEAD), F32)
                for u in range(sub):
                    kblk = _key_rows(k_ref, knew_ref, r0, u * HEAD, slice(hh * HEAD, (hh + 1) * HEAD))
                    kt = kblk.astype(BF16).astype(F32).T
                    per_key = jnp.maximum(per_key, jnp.sum(kt * kt, axis=0, keepdims=True))
                    kpos = (r0 + u * HEAD + col).astype(F32)
                    col_bias = _split3(slope2 * kpos)
                    lanes = slice(u * HEAD, (u + 1) * HEAD)
                    kt_ref[hh, 0, jb, :, lanes] = jnp.where(
                        row < dq, kt, _bias_lanes(row - dq, col_bias, True)).astype(BF16)
                    kt_ref[hh, 1, jb, :, lanes] = jnp.where(
                        row >= dq, kt, _bias_lanes(row, col_bias, True)).astype(BF16)
                stats_ref[hh, 0:1, :] = jnp.where(blk == jb, jnp.max(per_key, axis=1, keepdims=True),
                                                  stats_ref[hh, 0:1, :])
                return carry
            _for_each_key_block(nkb, fill)

    def first_needed_block(hh, slope2, n_before):
        q32 = q_ref[:, hh * HEAD:(hh + 1) * HEAD].astype(F32)
        nq2 = jnp.max(jnp.sum(q32 * q32, axis=1, keepdims=True), axis=0, keepdims=True)
        reach = jnp.sqrt(nq2 * stats_ref[hh, 0:1, :])
        own = jnp.max(jnp.where(blk == n_before, reach, 0.0), axis=1, keepdims=True)
        nearest = (past + i * tq - (blk + 1) * tk + 1).astype(F32)
        upper = reach - slope2 * nearest + own
        needed = jnp.logical_or(jnp.logical_not(upper < -UNDERFLOW_LOG2), blk >= n_before)
        return jnp.min(jnp.where(needed, blk, HEAD))

    lam = (jnp.exp(jnp.sum(lq1_ref[...] * lk1_ref[...], axis=1, keepdims=True))
           - jnp.exp(jnp.sum(lq2_ref[...] * lk2_ref[...], axis=1, keepdims=True)) + lam_init)

    lane = lax.broadcasted_iota(jnp.int32, (tq, HEAD), 1)
    qpos = past + i * tq + lax.broadcasted_iota(jnp.int32, (tq, 1), 0)
    qposf = qpos.astype(F32)
    qchunk = lax.shift_right_logical(qpos, CHUNK_SHIFT)
    kiota = lax.broadcasted_iota(jnp.int32, (1, tk), 1)
    aligned = tq == tk and past % tk == 0
    n_before, nk = _block_counts(past, i, tq, tk, k_valid)
    j_start = n_before if aligned else 0
    q_aug, slope_col = [], []
    for hh in range(hps):
        slope2 = LOG2E * slope_ref[hh][:, 0:1]
        slope_col.append(slope2)
        if aligned:
            j_start = jnp.minimum(j_start, first_needed_block(hh, slope2, n_before))
        row_bias = _split3(-slope2 * qposf)
        q = q_ref[:, hh * HEAD:(hh + 1) * HEAD].astype(F32)
        q1 = jnp.where(lane < dq, q, _bias_lanes(lane - dq, row_bias, False)).astype(BF16)
        q2 = jnp.where(lane >= dq, q, _bias_lanes(lane, row_bias, False)).astype(BF16)
        q_aug.append((q1, q2))

    def make_body(diagonal):
        def body(j, carry):
            rows = pl.ds(pl.multiple_of(j * tk, tk), tk)
            for hh in range(hps):
                vblk = vb_ref[rows, 2 * hh * HEAD:(2 * hh + 2) * HEAD]

                def fix(s, r0, strip, c0, cols, hh=hh):
                    kpos = j * tk + kiota[:, c0:c0 + cols]
                    ahead = jnp.maximum(kpos - qpos[r0:r0 + strip], 0).astype(F32)
                    visible = jnp.logical_and(
                        lax.shift_right_logical(kpos, CHUNK_SHIFT) <= qchunk[r0:r0 + strip],
                        kpos < k_valid)
                    return jnp.where(visible, s - (2.0 * slope_col[hh]) * ahead, -jnp.inf)

                for mp in range(2):
                    _online_softmax_block(2 * hh + mp, q_aug[hh][mp], kt_ref[hh, mp, j], vblk, refs,
                                          fix if diagonal else None, diagonal and aligned)
            return carry
        return body

    _softmax_init(refs)
    _loop_two_per_trip(j_start, n_before, make_body(False))
    lax.fori_loop(n_before, nk, make_body(True), 0)
    for hh in range(hps):
        o = _softmax_result(refs, 2 * hh) - lam * _softmax_result(refs, 2 * hh + 1)
        o_ref[:, hh * HEAD:(hh + 1) * HEAD] = (
            _rms(o, gn_ref[...]) * (1.0 - lam_init)).astype(o_ref.dtype)


def _diff(zb, k, v, new, slopes, lq1, lk1, lq2, lk2, gn, *, batch, heads, tq_total, tk_total, past,
          tq, tk, lam_init):
    _, n, d = zb.shape
    hps = HEADS_PER_STEP
    nq = tq_total // tq
    nkb = tk_total // tk
    kv_arrays, kv_specs = _kv_operands(k, v, new, batch, hps)
    lam_spec = _resident((1, HEAD // 2))
    return pl.pallas_call(
        functools.partial(_diff_kernel, past=past, k_valid=past + tq_total, tk=tk, lam_init=lam_init,
                          has_new=new is not None),
        grid=(batch, heads // hps, nq),
        in_specs=[
            pl.BlockSpec((None, tq, hps * HEAD), lambda b, h, i: (ZB_DQ, b * nq + i, h)),
            *kv_specs,
            pl.BlockSpec((hps, 1, HEAD), lambda b, h, i: (h, 0, 0)),
            lam_spec, lam_spec, lam_spec, lam_spec,
            _resident((1, HEAD)),
        ],
        out_specs=pl.BlockSpec((tq, hps * HEAD), lambda b, h, i: (b * nq + i, h)),
        out_shape=jax.ShapeDtypeStruct((n, d), BF16),
        scratch_shapes=[pltpu.VMEM((hps, 2, nkb, HEAD, tk), BF16),
                        pltpu.VMEM((tk_total, 2 * hps * HEAD), BF16),
                        pltpu.VMEM((hps, 8, HEAD), F32)] + _softmax_scratch(2 * hps, tq, tk),
        compiler_params=_params("parallel", "parallel", "arbitrary"),
        name="diff_attn",
    )(zb, *kv_arrays, slopes, lq1, lk1, lq2, lk2, gn)


def _merge_kernel(a1_ref, a2_ref, a3_ref, g1_ref, g2_ref, g3_ref, x_ref,
                  w1_ref, w2_ref, w3_ref, wo_ref, o_ref):
    def branch(a_ref, g_ref, w_ref):
        return _sigmoid(g_ref[...].astype(F32)) * jnp.dot(a_ref[...], w_ref[...],
                                                          preferred_element_type=F32)

    mix = branch(a1_ref, g1_ref, w1_ref) + branch(a2_ref, g2_ref, w2_ref) + branch(a3_ref, g3_ref, w3_ref)
    y = jnp.dot(mix.astype(BF16), wo_ref[...], preferred_element_type=F32)
    o_ref[...] = x_ref[...] + y


def _merge(a1, a2, a3, zb, x, w1, w2, w3, wo, *, tm):
    n, d = x.shape
    act = pl.BlockSpec((tm, d), lambda i: (i, 0))

    def gate(slot):
        return pl.BlockSpec((None, tm, d), lambda i: (slot, i, 0))

    wspec = _resident((d, d))
    return pl.pallas_call(
        _merge_kernel,
        grid=(n // tm,),
        in_specs=[act, act, act, gate(ZB_GHG), gate(ZB_GFOX), gate(ZB_GDIFF), act,
                  wspec, wspec, wspec, wspec],
        out_specs=act,
        out_shape=jax.ShapeDtypeStruct((n, d), F32),
        compiler_params=_params("parallel"),
        name="merge_out",
    )(a1, a2, a3, zb, zb, zb, x, w1, w2, w3, wo)


def _ffn_kernel(x_ref, g_ref, wgu_ref, wd_ref, gout_ref, o_ref, *, d_ff, fchunk, norm_out):
    x = x_ref[...]
    h = _rms(x, g_ref[...]).astype(BF16)
    acc = x
    for c0 in range(0, d_ff, fchunk):
        a = jnp.dot(h, wgu_ref[:, c0:c0 + fchunk], preferred_element_type=F32)
        b = jnp.dot(h, wgu_ref[:, d_ff + c0:d_ff + c0 + fchunk], preferred_element_type=F32)
        act = (a * _sigmoid(a) * b).astype(BF16)
        acc = acc + jnp.dot(act, wd_ref[c0:c0 + fchunk, :], preferred_element_type=F32)
    o_ref[...] = _rms(acc, gout_ref[...]) if norm_out else acc


def _ffn(x, g, wgu, wd, g_out, *, tm, norm_out):
    n, d = x.shape
    d_ff = wd.shape[0]
    fchunk = d_ff // 2
    assert fchunk % HEAD == 0
    act = pl.BlockSpec((tm, d), lambda i: (i, 0))
    return pl.pallas_call(
        functools.partial(_ffn_kernel, d_ff=d_ff, fchunk=fchunk, norm_out=norm_out),
        grid=(n // tm,),
        in_specs=[act, _resident((1, d)), _resident(wgu.shape), _resident(wd.shape), _resident((1, d))],
        out_specs=act,
        out_shape=jax.ShapeDtypeStruct((n, d), F32),
        compiler_params=_params("parallel"),
        name="ffn",
    )(x, g, wgu, wd, g_out)


def _prep_layer_weights(w_in_l, d):
    hd = HEADS
    splits = (d, d, d, d, d, d, d, hd, d, d, d, d, d, d)
    offs = [0]
    for s in splits:
        offs.append(offs[-1] + s)
    cols = [w_in_l[:, offs[k]:offs[k + 1]] for k in range(len(splits))]
    hq, hf, hi, hog, fq, fk, fv, ff, dq, dk, dv, g_hg, g_fox, g_diff = cols
    fq = fq * (HEAD ** -0.5 * LOG2E)
    dq = dq * ((HEAD // 2) ** -0.5 * LOG2E)
    w = jnp.concatenate([hq, hf, hi, hog, fq, dq, g_hg, g_fox, g_diff], axis=1).astype(BF16)
    w_cache = jnp.concatenate([fk, fv, dk, dv], axis=1).astype(BF16)
    wff = jnp.pad(ff, ((0, 0), (0, HEAD - hd))).astype(BF16)
    return w, w_cache, wff


def _pick_tile(n, want):
    t = min(n, want)
    assert n % t == 0
    return t


def _trunk(x, past, wts, lb_all, slopes):
    batch, seq, d = x.shape
    n = batch * seq
    heads = d // HEAD
    bh = batch * heads
    depth = len(wts)
    prompt = past is None
    plen = 0 if prompt else past[0].shape[2]
    tm = _pick_tile(n, 512)
    tm_proj = _pick_tile(n, 1024)

    if prompt:
        chunk, tt = CHUNK, _pick_tile(seq, 512)
        tq = tk = _pick_tile(seq, 512)
        tk_total = seq
    else:
        chunk, tt = seq, seq
        tq = seq
        tk = tk_total = -(-(plen + seq) // HEAD) * HEAD
    assert plen % tq == 0
    pad = tk_total - (plen + seq)

    xf = x.reshape(n, d)
    if not prompt:
        cached = {c: past[c].reshape(depth, batch * plen, d) for c in (0, 1, 3, 4)}
    states = []
    for l in range(depth):
        w = wts[l]
        lam_init = 0.8 - 0.6 * math.exp(-0.3 * l)
        zf, zb, fl = _proj_in(xf, w["norm_mix"], w["w_in"], w["w_ff"], w["f_bias"], tm=tm_proj)
        fk, fv, dk, dv, zs = _proj_cache(xf, w["norm_mix"], w["w_cache"], tm=tm)
        fox_logf = fl[:, :heads].reshape(batch, seq, heads)

        if prompt:
            s0 = jnp.zeros((batch, heads, HEAD, HEAD), F32)
        else:
            s0 = past[5][l]
        hg_o, hg_s = _hgrn(zf, zb, lb_all[l][None, :], w["hg_norm"], s0,
                           batch=batch, seq=seq, chunk=chunk, tt=tt)

        if prompt:
            fox_k, fox_v, fox_new = (zs, ZS_FK), (zs, ZS_FV), None
            diff_k, diff_v, diff_new = (zs, ZS_DK), (zs, ZS_DV), None
            logf_all = fox_logf
        else:
            fox_k, fox_v, fox_new = (cached[0], l), (cached[1], l), (zs, ZS_FK, ZS_FV)
            diff_k, diff_v, diff_new = (cached[3], l), (cached[4], l), (zs, ZS_DK, ZS_DV)
            logf_all = jnp.pad(jnp.concatenate([past[2][l], fox_logf], axis=1),
                               ((0, 0), (0, pad), (0, 0)))
        c_rows = _cumsum_time(logf_all.transpose(1, 0, 2).reshape(tk_total, bh))
        c_cols = c_rows.T.reshape(bh, tk_total // tk, 1, tk)
        fox_o = _fox(zb, fox_k, fox_v, fox_new, c_rows, c_cols, batch=batch, heads=heads,
                     tq_total=seq, tk_total=tk_total, past=plen, tq=tq, tk=tk)

        diff_o = _diff(zb, diff_k, diff_v, diff_new, slopes, w["lq1"], w["lk1"], w["lq2"], w["lk2"],
                       w["diff_norm"], batch=batch, heads=heads, tq_total=seq, tk_total=tk_total,
                       past=plen, tq=tq, tk=tk, lam_init=lam_init)

        xf = _merge(hg_o, fox_o, diff_o, zb, xf, w["w_br_hgrn"], w["w_br_fox"], w["w_br_diff"],
                    w["w_out"], tm=tm)
        xf = _ffn(xf, w["norm_ffn"], w["w_gate_up"], w["w_down"], w["norm_final"], tm=tm,
                  norm_out=l == depth - 1)

        def heads4(a):
            return a.reshape(batch, seq, heads, HEAD)
        states.append((heads4(fk), heads4(fv), fox_logf, heads4(dk), heads4(dv), hg_s))

    stacked = tuple(jnp.stack(s, axis=0) for s in zip(*states))
    return xf.reshape(batch, seq, d), stacked


def kernel(x_prompt, x_sample, cache_fox_k, cache_fox_v, cache_fox_logf, cache_diff_k, cache_diff_v,
           state_hgrn, norm_mix, w_in, hg_lb_logits, hg_norm, fox_f_bias, diff_lam_q1, diff_lam_k1,
           diff_lam_q2, diff_lam_k2, diff_norm, w_br_hgrn, w_br_fox, w_br_diff, w_out, norm_ffn,
           w_gate_up, w_down, norm_final):
    depth, d, _ = w_in.shape
    wts = []
    for l in range(depth):
        w, w_cache, wff = _prep_layer_weights(w_in[l], d)
        wts.append(dict(
            norm_mix=norm_mix[l][None, :], w_in=w, w_cache=w_cache, w_ff=wff,
            f_bias=jnp.pad(fox_f_bias[l], (0, HEAD - HEADS))[None, :],
            hg_norm=hg_norm[l][None, :],
            lq1=diff_lam_q1[l][None, :], lk1=diff_lam_k1[l][None, :],
            lq2=diff_lam_q2[l][None, :], lk2=diff_lam_k2[l][None, :],
            diff_norm=diff_norm[l][None, :],
            w_br_hgrn=w_br_hgrn[l].astype(BF16), w_br_fox=w_br_fox[l].astype(BF16),
            w_br_diff=w_br_diff[l].astype(BF16), w_out=w_out[l].astype(BF16),
            norm_ffn=norm_ffn[l][None, :],
            w_gate_up=w_gate_up[l].astype(BF16), w_down=w_down[l].astype(BF16),
            norm_final=norm_final[None, :],
        ))
    cs = jnp.cumsum(jax.nn.softmax(hg_lb_logits.astype(F32), axis=0), axis=0)
    lb_all = cs - cs[0:1]
    slope = jnp.exp2(-8.0 * jnp.arange(1, HEADS + 1, dtype=F32) / HEADS)
    slopes = jnp.broadcast_to(slope[:, None, None], (HEADS, 1, HEAD))

    y_p, st_p = _trunk(x_prompt, None, wts, lb_all, slopes)
    past = (cache_fox_k, cache_fox_v, cache_fox_logf, cache_diff_k, cache_diff_v, state_hgrn)
    y_s, st_s = _trunk(x_sample, past, wts, lb_all, slopes)
    return (y_p, y_s) + st_p + st_s
```
